```python
import math
import jax, jax.numpy as jnp
from jax import lax
import numpy as np

D_MODEL = 1024
BATCH = 32
SEQ = 2048
DEPTH = 2

D_MIX = D_MODEL
D_FF = 2816
EPS = 1e-6

CONV_W = D_MIX // 4
CONV_GROUPS = 4
CONV_K = 3

MLA_HEADS = 8
NOPE_DIM = 64
ROPE_DIM = 32
V_DIM = 64
Q_RANK = 256
KV_RANK = 128
QK_DIM = NOPE_DIM + ROPE_DIM
MLA_W = MLA_HEADS * V_DIM
ROPE_THETA = 10000.0
Q_BLOCK = 128

GMLP_W = D_MIX - CONV_W - MLA_W
GMLP_HEADS = 4
GMLP_HD = GMLP_W // GMLP_HEADS
CHUNK = 128

SPLITS = np.cumsum([CONV_W, CONV_W, CONV_W, Q_RANK, KV_RANK, ROPE_DIM, GMLP_W]).tolist()
N_IN = 3 * CONV_W + Q_RANK + KV_RANK + ROPE_DIM + 2 * GMLP_W

kernel_name = "hybrid_macaron_parallel_heads_encoder"


def rms_norm(x, g):
    xf = x.astype(jnp.float32)
    y = xf * lax.rsqrt(jnp.mean(xf * xf, axis=-1, keepdims=True) + EPS)
    return (y * g.astype(jnp.float32)).astype(x.dtype)


def swiglu_ffn(x, w_gu, w_down):
    gate, up = jnp.split(x @ w_gu, 2, axis=-1)
    return (jax.nn.silu(gate) * up) @ w_down


def rope_tables(seq, dim):
    pos = jnp.arange(seq, dtype=jnp.float32)
    inv = 1.0 / (ROPE_THETA ** (jnp.arange(0, dim, 2, dtype=jnp.float32) / dim))
    ang = pos[:, None] * inv[None, :]
    return jnp.cos(ang), jnp.sin(ang)


def apply_rope(x, cos, sin):
    cos = cos.astype(x.dtype)
    sin = sin.astype(x.dtype)
    x1, x2 = jnp.split(x, 2, axis=-1)
    return jnp.concatenate([x1 * cos - x2 * sin, x2 * cos + x1 * sin], axis=-1)


def short_conv_mixer(xc, gate_b, gate_c, conv_w, conv_b):
    z = gate_c * xc
    w = conv_w.astype(z.dtype)[:, None, :]
    conv = lax.conv_general_dilated(
        z, w, window_strides=(1,), padding=[(CONV_K // 2, CONV_K // 2)],
        dimension_numbers=("NWC", "WIO", "NWC"), feature_group_count=CONV_W)
    return gate_b * (conv + conv_b)


def mla_mixer(cq, ckv, k_rope, q_norm_g, w_uq, kv_norm_g, w_ukv, cos, sin):
    bsz, seq, _ = cq.shape
    q = (rms_norm(cq, q_norm_g) @ w_uq).reshape(bsz, seq, MLA_HEADS, QK_DIM)
    q_nope = q[..., :NOPE_DIM]
    q_rope = apply_rope(q[..., NOPE_DIM:], cos[None, :, None, :], sin[None, :, None, :])
    kv = (rms_norm(ckv, kv_norm_g) @ w_ukv).reshape(bsz, seq, MLA_HEADS, NOPE_DIM + V_DIM)
    k_nope, v = kv[..., :NOPE_DIM], kv[..., NOPE_DIM:]
    k_rope = apply_rope(k_rope, cos[None], sin[None])
    scale = 1.0 / math.sqrt(QK_DIM)
    nb = seq // Q_BLOCK
    qn_blocks = (q_nope * scale).reshape(bsz, nb, Q_BLOCK, MLA_HEADS, NOPE_DIM).transpose(1, 0, 2, 3, 4)
    qr_blocks = (q_rope * scale).reshape(bsz, nb, Q_BLOCK, MLA_HEADS, ROPE_DIM).transpose(1, 0, 2, 3, 4)

    def attend(blk):
        qn, qr = blk
        s = (jnp.einsum("bqhd,bkhd->bhqk", qn, k_nope)
             + jnp.einsum("bqhr,bkr->bhqk", qr, k_rope)).astype(jnp.float32)
        p = jax.nn.softmax(s, axis=-1).astype(v.dtype)
        return jnp.einsum("bhqk,bkhd->bqhd", p, v)

    o = lax.map(attend, (qn_blocks, qr_blocks))
    return o.transpose(1, 0, 2, 3, 4).reshape(bsz, seq, MLA_W)


def gmlp_mixer(zu, zv, norm_g, ws, bias):
    u = jax.nn.gelu(zu, approximate=False)
    v = rms_norm(jax.nn.gelu(zv, approximate=False), norm_g)
    bsz, seq, _ = v.shape
    v = v.reshape(bsz, seq // CHUNK, CHUNK, GMLP_HEADS, GMLP_HD)
    mixed = jnp.einsum("gpq,bcqgd->bcpgd", ws, v) + bias.T[None, None, :, :, None]
    return u * mixed.reshape(bsz, seq, GMLP_W)


def setup_inputs(seed: int = 0) -> dict:
    key = jax.random.key(seed)
    ks = jax.random.split(key, 24)
    L = DEPTH

    def nrm(k, shape, scale):
        return jax.random.normal(k, shape, jnp.float32) * scale

    def gain(k, shape):
        return 1.0 + 0.02 * jax.random.normal(k, shape, jnp.float32)

    return {
        "x": jax.random.normal(ks[0], (BATCH, SEQ, D_MODEL), jnp.float32),
        "ffn1_pre_g": gain(ks[1], (L, D_MODEL)),
        "ffn1_w_gu": nrm(ks[2], (L, D_MODEL, 2 * D_FF), D_MODEL ** -0.5),
        "ffn1_w_down": nrm(ks[3], (L, D_FF, D_MODEL), D_FF ** -0.5),
        "ffn1_post_g": gain(ks[4], (L, D_MODEL)),
        "mix_pre_g": gain(ks[5], (L, D_MODEL)),
        "w_in": nrm(ks[6], (L, D_MODEL, N_IN), D_MODEL ** -0.5),
        "conv_w": nrm(ks[7], (L, CONV_K, CONV_W), CONV_K ** -0.5),
        "conv_b": nrm(ks[8], (L, CONV_W), 0.02),
        "q_norm_g": gain(ks[9], (L, Q_RANK)),
        "w_uq": nrm(ks[10], (L, Q_RANK, MLA_HEADS * QK_DIM), Q_RANK ** -0.5),
        "kv_norm_g": gain(ks[11], (L, KV_RANK)),
        "w_ukv": nrm(ks[12], (L, KV_RANK, MLA_HEADS * (NOPE_DIM + V_DIM)), KV_RANK ** -0.5),
        "gmlp_norm_g": gain(ks[13], (L, GMLP_W)),
        "gmlp_ws": nrm(ks[14], (L, GMLP_HEADS, CHUNK, CHUNK), CHUNK ** -0.5),
        "gmlp_b": gain(ks[15], (L, GMLP_HEADS, CHUNK)),
        "w_out": nrm(ks[16], (L, D_MIX, D_MODEL), D_MIX ** -0.5),
        "mix_post_g": gain(ks[17], (L, D_MODEL)),
        "ffn2_pre_g": gain(ks[18], (L, D_MODEL)),
        "ffn2_w_gu": nrm(ks[19], (L, D_MODEL, 2 * D_FF), D_MODEL ** -0.5),
        "ffn2_w_down": nrm(ks[20], (L, D_FF, D_MODEL), D_FF ** -0.5),
        "ffn2_post_g": gain(ks[21], (L, D_MODEL)),
    }


def reference(x, ffn1_pre_g, ffn1_w_gu, ffn1_w_down, ffn1_post_g, mix_pre_g, w_in,
              conv_w, conv_b, q_norm_g, w_uq, kv_norm_g, w_ukv, gmlp_norm_g, gmlp_ws,
              gmlp_b, w_out, mix_post_g, ffn2_pre_g, ffn2_w_gu, ffn2_w_down, ffn2_post_g):
    cos, sin = rope_tables(x.shape[1], ROPE_DIM)
    for l in range(DEPTH):
        h = swiglu_ffn(rms_norm(x, ffn1_pre_g[l]), ffn1_w_gu[l], ffn1_w_down[l])
        x = x + 0.5 * rms_norm(h, ffn1_post_g[l])

        n = rms_norm(x, mix_pre_g[l])
        z = n @ w_in[l]
        xc, gb, gc, cq, ckv, kr, zu, zv = jnp.split(z, SPLITS, axis=-1)
        y_conv = short_conv_mixer(xc, gb, gc, conv_w[l], conv_b[l])
        y_mla = mla_mixer(cq, ckv, kr, q_norm_g[l], w_uq[l], kv_norm_g[l], w_ukv[l], cos, sin)
        y_gmlp = gmlp_mixer(zu, zv, gmlp_norm_g[l], gmlp_ws[l], gmlp_b[l])
        y = jnp.concatenate([y_conv, y_mla, y_gmlp], axis=-1) @ w_out[l]
        x = x + rms_norm(y, mix_post_g[l])

        h = swiglu_ffn(rms_norm(x, ffn2_pre_g[l]), ffn2_w_gu[l], ffn2_w_down[l])
        x = x + 0.5 * rms_norm(h, ffn2_post_g[l])
    return x
```

```python
import functools
import math

import jax
import jax.numpy as jnp
import numpy as np
from jax import lax
from jax.experimental import pallas as pl
from jax.experimental.pallas import tpu as pltpu

F32 = jnp.float32
BF16 = jnp.bfloat16

D_MODEL = 1024
D_FF = 2816
EPS = 1e-6
CONV_W = 256
MLA_HEADS = 8
NOPE_DIM = 64
ROPE_DIM = 32
V_DIM = 64
Q_RANK = 256
KV_RANK = 128
QK_DIM = NOPE_DIM + ROPE_DIM
MLA_W = MLA_HEADS * V_DIM
ROPE_THETA = 10000.0
GMLP_W = 256
GMLP_HEADS = 4
GMLP_HD = 64
CHUNK = 128
N_IN = 3 * CONV_W + Q_RANK + KV_RANK + ROPE_DIM + 2 * GMLP_W

LANES = 128
HEAD_PAD = LANES
C_XC, C_GB, C_GC, C_CQ, C_ZU, C_ZV, C_CKV, C_KR = 0, 256, 512, 768, 1024, 1280, 1536, 1664
N_IN_PAD = 1792

TM_FFN = 512
TM_PROJ = 512
TQ_MIX = 256
VMEM_LIMIT = 56 * 1024 * 1024


def _rms(x, g):
    return x * lax.rsqrt(jnp.mean(x * x, axis=-1, keepdims=True) + EPS) * g


def _gelu(x):
    return 0.5 * x * (1.0 + lax.erf(x * math.sqrt(0.5)))


def _const_spec(shape):
    nd = len(shape)
    return pl.BlockSpec(shape, lambda *_: (0,) * nd, pipeline_mode=pl.Buffered(1))


def _ffn_body(x_ref, pre_ref, wgu_ref, wd_ref, post_ref, o_ref):
    x = x_ref[...]
    xn = _rms(x, pre_ref[...]).astype(BF16)
    gu = jnp.dot(xn, wgu_ref[...], preferred_element_type=F32)
    g = gu[:, :D_FF]
    u = gu[:, D_FF:]
    h = (jax.nn.silu(g) * u).astype(BF16)
    d = jnp.dot(h, wd_ref[...], preferred_element_type=F32)
    o_ref[...] = x + 0.5 * _rms(d, post_ref[...])


def _ffn(x, pre_g, w_gu, w_down, post_g):
    t = x.shape[0]
    tm = min(TM_FFN, t)
    return pl.pallas_call(
        _ffn_body,
        grid=(t // tm,),
        in_specs=[
            pl.BlockSpec((tm, D_MODEL), lambda i: (i, 0)),
            _const_spec((1, D_MODEL)),
            _const_spec((D_MODEL, 2 * D_FF)),
            _const_spec((D_FF, D_MODEL)),
            _const_spec((1, D_MODEL)),
        ],
        out_specs=pl.BlockSpec((tm, D_MODEL), lambda i: (i, 0)),
        out_shape=jax.ShapeDtypeStruct((t, D_MODEL), F32),
        compiler_params=pltpu.CompilerParams(
            dimension_semantics=("parallel",), vmem_limit_bytes=VMEM_LIMIT),
        name="ffn",
    )(x, pre_g, w_gu, w_down, post_g)


def _proj_body(x_ref, pre_ref, win_ref, qg_ref, wqt_ref, wqswt_ref, kvg_ref,
               wk_ref, wvt_ref, gg_ref, wscat_ref, gbias_ref, gmask_ref,
               ktab_ref, qcos_ref, qsin_ref,
               p_ref, gb_ref, ygm_ref, qt_ref, k_ref, vt_ref):
    tm = x_ref.shape[0]
    n = _rms(x_ref[...], pre_ref[...]).astype(BF16)
    z = jnp.dot(n, win_ref[...], preferred_element_type=F32)

    p_ref[...] = z[:, C_GC:C_GC + CONV_W] * z[:, C_XC:C_XC + CONV_W]
    gb_ref[...] = z[:, C_GB:C_GB + CONV_W]

    scale = 1.0 / math.sqrt(QK_DIM)
    cqn = _rms(z[:, C_CQ:C_CQ + Q_RANK], qg_ref[...]).astype(BF16)
    nt = (((1,), (1,)), ((), ()))
    qt = lax.dot_general(wqt_ref[...], cqn, nt, preferred_element_type=F32)
    qswt = lax.dot_general(wqswt_ref[...], cqn, nt, preferred_element_type=F32)
    qcos = qcos_ref[...]
    qsin = qsin_ref[...]
    zero_pad = jnp.zeros((HEAD_PAD - QK_DIM, tm), BF16)
    for h in range(MLA_HEADS):
        r0 = h * QK_DIM
        nope = qt[r0:r0 + NOPE_DIM] * scale
        rope = (qt[r0 + NOPE_DIM:r0 + QK_DIM] * qcos
                + qswt[h * ROPE_DIM:(h + 1) * ROPE_DIM] * qsin) * scale
        qt_ref[h, 0:NOPE_DIM, :] = nope.astype(BF16)
        qt_ref[h, NOPE_DIM:QK_DIM, :] = rope.astype(BF16)
        qt_ref[h, QK_DIM:HEAD_PAD, :] = zero_pad

    ckvn = _rms(z[:, C_CKV:C_CKV + KV_RANK], kvg_ref[...]).astype(BF16)
    r = z[:, C_KR:C_KR + LANES] * ktab_ref[...]
    r = r + pltpu.roll(r, LANES - ROPE_DIM, 1)
    kin = jnp.concatenate([ckvn, r.astype(BF16)], axis=1)
    kall = jnp.dot(kin, wk_ref[...], preferred_element_type=F32).astype(BF16)
    for h in range(MLA_HEADS):
        k_ref[h] = kall[:, h * HEAD_PAD:(h + 1) * HEAD_PAD]
    vt = lax.dot_general(wvt_ref[...], ckvn, nt, preferred_element_type=F32)
    vt_ref[...] = vt.astype(BF16)

    u = _gelu(z[:, C_ZU:C_ZU + GMLP_W])
    v = _rms(_gelu(z[:, C_ZV:C_ZV + GMLP_W]), gg_ref[...]).astype(BF16)
    wscat = wscat_ref[...]
    gbias = gbias_ref[...]
    gmask = gmask_ref[...] != 0
    for c in range(tm // CHUNK):
        vc = v[c * CHUNK:(c + 1) * CHUNK]
        rhs = jnp.where(gmask, jnp.concatenate([vc] * GMLP_HEADS, axis=0),
                        jnp.zeros((), BF16))
        mixed = jnp.dot(wscat, rhs, preferred_element_type=F32) + gbias
        ygm_ref[c * CHUNK:(c + 1) * CHUNK, :] = (
            u[c * CHUNK:(c + 1) * CHUNK] * mixed).astype(BF16)


def _proj(x, seq, pre_g, w_in, q_g, wq_t, wqsw_t, kv_g, w_k, wv_t, gm_g,
          ws_cat, g_bias, g_mask, k_tab, q_cos, q_sin):
    t = x.shape[0]
    tm = min(TM_PROJ, seq)
    per_seq = seq // tm
    tok = lambda i: (i, 0)
    return pl.pallas_call(
        _proj_body,
        grid=(t // tm,),
        in_specs=[
            pl.BlockSpec((tm, D_MODEL), tok),
            _const_spec((1, D_MODEL)),
            _const_spec((D_MODEL, N_IN_PAD)),
            _const_spec((1, Q_RANK)),
            _const_spec((MLA_HEADS * QK_DIM, Q_RANK)),
            _const_spec((MLA_HEADS * ROPE_DIM, Q_RANK)),
            _const_spec((1, KV_RANK)),
            _const_spec((2 * LANES, MLA_HEADS * HEAD_PAD)),
            _const_spec((MLA_W, KV_RANK)),
            _const_spec((1, GMLP_W)),
            _const_spec((CHUNK, GMLP_HEADS * CHUNK)),
            _const_spec((CHUNK, GMLP_W)),
            _const_spec((GMLP_HEADS * CHUNK, GMLP_W)),
            pl.BlockSpec((tm, LANES), lambda i: (i % per_seq, 0)),
            pl.BlockSpec((ROPE_DIM, tm), lambda i: (0, i % per_seq)),
            pl.BlockSpec((ROPE_DIM, tm), lambda i: (0, i % per_seq)),
        ],
        out_specs=[
            pl.BlockSpec((tm, CONV_W), tok),
            pl.BlockSpec((tm, CONV_W), tok),
            pl.BlockSpec((tm, GMLP_W), tok),
            pl.BlockSpec((MLA_HEADS, HEAD_PAD, tm), lambda i: (0, 0, i)),
            pl.BlockSpec((MLA_HEADS, tm, HEAD_PAD), lambda i: (0, i, 0)),
            pl.BlockSpec((MLA_W, tm), lambda i: (0, i)),
        ],
        out_shape=[
            jax.ShapeDtypeStruct((t, CONV_W), F32),
            jax.ShapeDtypeStruct((t, CONV_W), F32),
            jax.ShapeDtypeStruct((t, GMLP_W), BF16),
            jax.ShapeDtypeStruct((MLA_HEADS, HEAD_PAD, t), BF16),
            jax.ShapeDtypeStruct((MLA_HEADS, t, HEAD_PAD), BF16),
            jax.ShapeDtypeStruct((MLA_W, t), BF16),
        ],
        compiler_params=pltpu.CompilerParams(
            dimension_semantics=("parallel",), vmem_limit_bytes=VMEM_LIMIT),
        name="proj",
    )(x, pre_g, w_in, q_g, wq_t, wqsw_t, kv_g, w_k, wv_t, gm_g, ws_cat,
      g_bias, g_mask, k_tab, q_cos, q_sin)


def _mix_body(x_ref, p_ref, gb_ref, ygm_ref, qt_ref, k_ref, vt_ref,
              cw_ref, cb_ref, wout_ref, post_ref, o_ref, ot_ref):
    tq = x_ref.shape[0]
    seq = p_ref.shape[0]
    i = pl.program_id(1)
    r0 = pl.multiple_of(i * tq, tq)

    cur = p_ref[pl.ds(r0, tq), :]
    lo = pl.multiple_of(jnp.maximum(r0 - 8, 0), 8)
    hi = pl.multiple_of(jnp.minimum(r0 + tq, seq - 8), 8)
    before = p_ref[pl.ds(lo, 8), :][7:8, :]
    after = p_ref[pl.ds(hi, 8), :][0:1, :]
    before = jnp.where(i > 0, before, 0.0)
    after = jnp.where(i < pl.num_programs(1) - 1, after, 0.0)
    row = lax.broadcasted_iota(jnp.int32, (tq, CONV_W), 0)
    prev = jnp.where(row == 0, before, pltpu.roll(cur, 1, 0))
    nxt = jnp.where(row == tq - 1, after, pltpu.roll(cur, tq - 1, 0))
    cw = cw_ref[...]
    conv = prev * cw[0:1, :] + cur * cw[1:2, :] + nxt * cw[2:3, :]
    y_conv = gb_ref[...] * (conv + cb_ref[...])

    for h in range(MLA_HEADS):
        s = jnp.dot(k_ref[h], qt_ref[h], preferred_element_type=F32)
        m = jnp.max(s, axis=0, keepdims=True)
        e = jnp.exp(s - m)
        l = jnp.sum(e, axis=0, keepdims=True)
        o = jnp.dot(vt_ref[h * V_DIM:(h + 1) * V_DIM, :], e.astype(BF16),
                    preferred_element_type=F32)
        ot_ref[h * V_DIM:(h + 1) * V_DIM, :] = o / l
    y_mla = ot_ref[...].T

    ycat = jnp.concatenate(
        [y_conv.astype(BF16), y_mla.astype(BF16), ygm_ref[...]], axis=1)
    y = jnp.dot(ycat, wout_ref[...], preferred_element_type=F32)
    o_ref[...] = x_ref[...] + _rms(y, post_ref[...])


def _mix(x, batch, seq, p, gb, ygm, qt, k, vt, conv_w, conv_b, w_out, post_g):
    t = x.shape[0]
    tq = min(TQ_MIX, seq)
    nq = seq // tq
    tok = lambda b, i: (b * nq + i, 0)
    return pl.pallas_call(
        _mix_body,
        grid=(batch, nq),
        in_specs=[
            pl.BlockSpec((tq, D_MODEL), tok),
            pl.BlockSpec((seq, CONV_W), lambda b, i: (b, 0)),
            pl.BlockSpec((tq, CONV_W), tok),
            pl.BlockSpec((tq, GMLP_W), tok),
            pl.BlockSpec((MLA_HEADS, HEAD_PAD, tq), lambda b, i: (0, 0, b * nq + i)),
            pl.BlockSpec((MLA_HEADS, seq, HEAD_PAD), lambda b, i: (0, b, 0)),
            pl.BlockSpec((MLA_W, seq), lambda b, i: (0, b)),
            _const_spec((8, CONV_W)),
            _const_spec((1, CONV_W)),
            _const_spec((D_MODEL, D_MODEL)),
            _const_spec((1, D_MODEL)),
        ],
        out_specs=pl.BlockSpec((tq, D_MODEL), tok),
        out_shape=jax.ShapeDtypeStruct((t, D_MODEL), F32),
        scratch_shapes=[pltpu.VMEM((MLA_W, tq), F32)],
        compiler_params=pltpu.CompilerParams(
            dimension_semantics=("parallel", "arbitrary"),
            vmem_limit_bytes=VMEM_LIMIT),
        name="mix",
    )(x, p, gb, ygm, qt, k, vt, conv_w, conv_b, w_out, post_g)


def _rope_tables(seq):
    pos = jnp.arange(seq, dtype=F32)
    inv = 1.0 / (ROPE_THETA ** (jnp.arange(0, ROPE_DIM, 2, dtype=F32) / ROPE_DIM))
    ang = pos[:, None] * inv[None, :]
    cos, sin = jnp.cos(ang), jnp.sin(ang)
    cc = jnp.concatenate([cos, cos], axis=1)
    ss = jnp.concatenate([-sin, sin], axis=1)
    k_tab = jnp.concatenate(
        [cc, ss, jnp.zeros((seq, LANES - 2 * ROPE_DIM), F32)], axis=1)
    return k_tab, cc.T, ss.T


def _half_swap(w, axis):
    a, b = jnp.split(w, 2, axis=axis)
    return jnp.concatenate([b, a], axis=axis)


def _prep_layer(w_in, w_uq, w_ukv, gmlp_ws, gmlp_b, conv_w):
    s = np.cumsum([CONV_W, CONV_W, CONV_W, Q_RANK, KV_RANK, ROPE_DIM, GMLP_W]).tolist()
    xc, gb, gc, cq, ckv, kr, zu, zv = jnp.split(w_in, s, axis=1)
    pad = jnp.zeros((D_MODEL, N_IN_PAD - N_IN - ROPE_DIM), F32)
    w_in_p = jnp.concatenate(
        [xc, gb, gc, cq, zu, zv, ckv, kr, _half_swap(kr, 1), pad], axis=1).astype(BF16)

    wq = w_uq.reshape(Q_RANK, MLA_HEADS, QK_DIM)
    wq_t = w_uq.T.astype(BF16)
    wqsw_t = _half_swap(wq[:, :, NOPE_DIM:], 2).reshape(
        Q_RANK, MLA_HEADS * ROPE_DIM).T.astype(BF16)

    wkv = w_ukv.reshape(KV_RANK, MLA_HEADS, NOPE_DIM + V_DIM)
    wk_nope = jnp.pad(wkv[:, :, :NOPE_DIM], ((0, 0), (0, 0), (0, HEAD_PAD - NOPE_DIM)))
    place = jnp.zeros((LANES, MLA_HEADS, HEAD_PAD), F32)
    j = jnp.arange(ROPE_DIM)
    place = place.at[j, :, NOPE_DIM + j].set(1.0)
    w_k = jnp.concatenate([wk_nope, place], axis=0).reshape(
        2 * LANES, MLA_HEADS * HEAD_PAD).astype(BF16)
    wv_t = wkv[:, :, NOPE_DIM:].reshape(KV_RANK, MLA_W).T.astype(BF16)

    ws_cat = jnp.transpose(gmlp_ws, (1, 0, 2)).reshape(
        CHUNK, GMLP_HEADS * CHUNK).astype(BF16)
    g_bias = jnp.repeat(gmlp_b.T, GMLP_HD, axis=1)
    rg = jnp.arange(GMLP_HEADS * CHUNK)[:, None] // CHUNK
    cg = jnp.arange(GMLP_W)[None, :] // GMLP_HD
    g_mask = (rg == cg).astype(BF16)
    conv_w8 = jnp.pad(conv_w, ((0, 8 - conv_w.shape[0]), (0, 0)))
    return w_in_p, wq_t, wqsw_t, w_k, wv_t, ws_cat, g_bias, g_mask, conv_w8


def kernel(x, ffn1_pre_g, ffn1_w_gu, ffn1_w_down, ffn1_post_g, mix_pre_g, w_in,
           conv_w, conv_b, q_norm_g, w_uq, kv_norm_g, w_ukv, gmlp_norm_g, gmlp_ws,
           gmlp_b, w_out, mix_post_g, ffn2_pre_g, ffn2_w_gu, ffn2_w_down, ffn2_post_g):
    batch, seq, _ = x.shape
    depth = w_in.shape[0]
    k_tab, q_cos, q_sin = _rope_tables(seq)
    row = lambda g: g.reshape(1, -1)
    h = x.reshape(batch * seq, D_MODEL)
    for l in range(depth):
        (w_in_p, wq_t, wqsw_t, w_k, wv_t, ws_cat, g_bias, g_mask, conv_w8) = _prep_layer(
            w_in[l], w_uq[l], w_ukv[l], gmlp_ws[l], gmlp_b[l], conv_w[l])
        h = _ffn(h, row(ffn1_pre_g[l]), ffn1_w_gu[l].astype(BF16),
                 ffn1_w_down[l].astype(BF16), row(ffn1_post_g[l]))
        p, gb, ygm, qt, k, vt = _proj(
            h, seq, row(mix_pre_g[l]), w_in_p, row(q_norm_g[l]), wq_t, wqsw_t,
            row(kv_norm_g[l]), w_k, wv_t, row(gmlp_norm_g[l]), ws_cat, g_bias,
            g_mask, k_tab, q_cos, q_sin)
        h = _mix(h, batch, seq, p, gb, ygm, qt, k, vt, conv_w8, row(conv_b[l]),
                 w_out[l].astype(BF16), row(mix_post_g[l]))
        h = _ffn(h, row(ffn2_pre_g[l]), ffn2_w_gu[l].astype(BF16),
                 ffn2_w_down[l].astype(BF16), row(ffn2_post_g[l]))
    return h.reshape(batch, seq, D_MODEL)
```

```python
import functools
import math

import jax
import jax.numpy as jnp
import numpy as np
from jax import lax
from jax.experimental import pallas as pl
from jax.experimental.pallas import tpu as pltpu

F32 = jnp.float32
BF16 = jnp.bfloat16

D_MODEL = 1024
D_FF = 2816
EPS = 1e-6
CONV_W = 256
MLA_HEADS = 8
NOPE_DIM = 64
ROPE_DIM = 32
V_DIM = 64
Q_RANK = 256
KV_RANK = 128
QK_DIM = NOPE_DIM + ROPE_DIM
MLA_W = MLA_HEADS * V_DIM
ROPE_THETA = 10000.0
GMLP_W = 256
GMLP_HEADS = 4
GMLP_HD = 64
CHUNK = 128
N_IN = 3 * CONV_W + Q_RANK + KV_RANK + ROPE_DIM + 2 * GMLP_W

LANES = 128
HEAD_PAD = LANES
V_PAD = V_DIM + 16
C_XC, C_GB, C_GC, C_CQ, C_ZU, C_ZV, C_CKV, C_KR = 0, 256, 512, 768, 1024, 1280, 1536, 1664
N_IN_PAD = 1792

TM_FFN = 512
TM_PROJ = 512
TQ_MIX = 256
KEY_CHUNK = 256
SCORE_LOOKAHEAD = 6
VMEM_LIMIT = 56 * 1024 * 1024


def _rms(x, g):
    return x * lax.rsqrt(jnp.mean(x * x, axis=-1, keepdims=True) + EPS) * g


def _gelu(x):
    return 0.5 * x * (1.0 + lax.erf(x * math.sqrt(0.5)))


def _const_spec(shape):
    nd = len(shape)
    return pl.BlockSpec(shape, lambda *_: (0,) * nd, pipeline_mode=pl.Buffered(1))


def _ffn_body(x_ref, pre_ref, wgu_ref, wd_ref, post_ref, o_ref):
    x = x_ref[...]
    xn = _rms(x, pre_ref[...]).astype(BF16)
    gu = jnp.dot(xn, wgu_ref[...], preferred_element_type=F32)
    g = gu[:, :D_FF]
    u = gu[:, D_FF:]
    h = (jax.nn.silu(g) * u).astype(BF16)
    d = jnp.dot(h, wd_ref[...], preferred_element_type=F32)
    o_ref[...] = x + 0.5 * _rms(d, post_ref[...])


def _ffn(x, pre_g, w_gu, w_down, post_g):
    t = x.shape[0]
    tm = min(TM_FFN, t)
    return pl.pallas_call(
        _ffn_body,
        grid=(t // tm,),
        in_specs=[
            pl.BlockSpec((tm, D_MODEL), lambda i: (i, 0)),
            _const_spec((1, D_MODEL)),
            _const_spec((D_MODEL, 2 * D_FF)),
            _const_spec((D_FF, D_MODEL)),
            _const_spec((1, D_MODEL)),
        ],
        out_specs=pl.BlockSpec((tm, D_MODEL), lambda i: (i, 0)),
        out_shape=jax.ShapeDtypeStruct((t, D_MODEL), F32),
        compiler_params=pltpu.CompilerParams(
            dimension_semantics=("parallel",), vmem_limit_bytes=VMEM_LIMIT),
        name="ffn",
    )(x, pre_g, w_gu, w_down, post_g)


def _proj_body(x_ref, pre_ref, win_ref, qg_ref, wqt_ref, wqswt_ref, kvg_ref,
               wk_ref, wvt_ref, gg_ref, wscat_ref, gbias_ref, gmask_ref,
               ktab_ref, qcos_ref, qsin_ref,
               p_ref, gb_ref, ygm_ref, qt_ref, k_ref, vt_ref):
    tm = x_ref.shape[0]
    n = _rms(x_ref[...], pre_ref[...]).astype(BF16)
    z = jnp.dot(n, win_ref[...], preferred_element_type=F32)

    p_ref[...] = z[:, C_GC:C_GC + CONV_W] * z[:, C_XC:C_XC + CONV_W]
    gb_ref[...] = z[:, C_GB:C_GB + CONV_W]

    scale = math.log2(math.e) / math.sqrt(QK_DIM)
    cqn = _rms(z[:, C_CQ:C_CQ + Q_RANK], qg_ref[...]).astype(BF16)
    nt = (((1,), (1,)), ((), ()))
    qt = lax.dot_general(wqt_ref[...], cqn, nt, preferred_element_type=F32)
    qswt = lax.dot_general(wqswt_ref[...], cqn, nt, preferred_element_type=F32)
    qcos = qcos_ref[...]
    qsin = qsin_ref[...]
    zero_pad = jnp.zeros((HEAD_PAD - QK_DIM, tm), BF16)
    for h in range(MLA_HEADS):
        r0 = h * QK_DIM
        nope = qt[r0:r0 + NOPE_DIM] * scale
        rope = (qt[r0 + NOPE_DIM:r0 + QK_DIM] * qcos
                + qswt[h * ROPE_DIM:(h + 1) * ROPE_DIM] * qsin) * scale
        qt_ref[h, 0:NOPE_DIM, :] = nope.astype(BF16)
        qt_ref[h, NOPE_DIM:QK_DIM, :] = rope.astype(BF16)
        qt_ref[h, QK_DIM:HEAD_PAD, :] = zero_pad

    ckvn = _rms(z[:, C_CKV:C_CKV + KV_RANK], kvg_ref[...]).astype(BF16)
    r = z[:, C_KR:C_KR + LANES] * ktab_ref[...]
    r = r + pltpu.roll(r, LANES - ROPE_DIM, 1)
    kin = jnp.concatenate([ckvn, r.astype(BF16)], axis=1)
    kall = jnp.dot(kin, wk_ref[...], preferred_element_type=F32).astype(BF16)
    for h in range(MLA_HEADS):
        k_ref[h] = kall[:, h * HEAD_PAD:(h + 1) * HEAD_PAD]
    vt = lax.dot_general(wvt_ref[...], ckvn, nt, preferred_element_type=F32)
    ones_rows = (lax.broadcasted_iota(jnp.int32, (V_PAD - V_DIM, tm), 0) == 0).astype(BF16)
    for h in range(MLA_HEADS):
        vt_ref[h, 0:V_DIM, :] = vt[h * V_DIM:(h + 1) * V_DIM].astype(BF16)
        vt_ref[h, V_DIM:V_PAD, :] = ones_rows

    u = _gelu(z[:, C_ZU:C_ZU + GMLP_W])
    v = _rms(_gelu(z[:, C_ZV:C_ZV + GMLP_W]), gg_ref[...]).astype(BF16)
    wscat = wscat_ref[...]
    gbias = gbias_ref[...]
    gmask = gmask_ref[...] != 0
    for c in range(tm // CHUNK):
        vc = v[c * CHUNK:(c + 1) * CHUNK]
        rhs = jnp.where(gmask, jnp.concatenate([vc] * GMLP_HEADS, axis=0),
                        jnp.zeros((), BF16))
        mixed = jnp.dot(wscat, rhs, preferred_element_type=F32) + gbias
        ygm_ref[c * CHUNK:(c + 1) * CHUNK, :] = (
            u[c * CHUNK:(c + 1) * CHUNK] * mixed).astype(BF16)


def _proj(x, seq, pre_g, w_in, q_g, wq_t, wqsw_t, kv_g, w_k, wv_t, gm_g,
          ws_cat, g_bias, g_mask, k_tab, q_cos, q_sin):
    t = x.shape[0]
    tm = min(TM_PROJ, seq)
    per_seq = seq // tm
    tok = lambda i: (i, 0)
    return pl.pallas_call(
        _proj_body,
        grid=(t // tm,),
        in_specs=[
            pl.BlockSpec((tm, D_MODEL), tok),
            _const_spec((1, D_MODEL)),
            _const_spec((D_MODEL, N_IN_PAD)),
            _const_spec((1, Q_RANK)),
            _const_spec((MLA_HEADS * QK_DIM, Q_RANK)),
            _const_spec((MLA_HEADS * ROPE_DIM, Q_RANK)),
            _const_spec((1, KV_RANK)),
            _const_spec((2 * LANES, MLA_HEADS * HEAD_PAD)),
            _const_spec((MLA_W, KV_RANK)),
            _const_spec((1, GMLP_W)),
            _const_spec((CHUNK, GMLP_HEADS * CHUNK)),
            _const_spec((CHUNK, GMLP_W)),
            _const_spec((GMLP_HEADS * CHUNK, GMLP_W)),
            pl.BlockSpec((tm, LANES), lambda i: (i % per_seq, 0)),
            pl.BlockSpec((ROPE_DIM, tm), lambda i: (0, i % per_seq)),
            pl.BlockSpec((ROPE_DIM, tm), lambda i: (0, i % per_seq)),
        ],
        out_specs=[
            pl.BlockSpec((tm, CONV_W), tok),
            pl.BlockSpec((tm, CONV_W), tok),
            pl.BlockSpec((tm, GMLP_W), tok),
            pl.BlockSpec((MLA_HEADS, HEAD_PAD, tm), lambda i: (0, 0, i)),
            pl.BlockSpec((MLA_HEADS, tm, HEAD_PAD), lambda i: (0, i, 0)),
            pl.BlockSpec((MLA_HEADS, V_PAD, tm), lambda i: (0, 0, i)),
        ],
        out_shape=[
            jax.ShapeDtypeStruct((t, CONV_W), F32),
            jax.ShapeDtypeStruct((t, CONV_W), F32),
            jax.ShapeDtypeStruct((t, GMLP_W), BF16),
            jax.ShapeDtypeStruct((MLA_HEADS, HEAD_PAD, t), BF16),
            jax.ShapeDtypeStruct((MLA_HEADS, t, HEAD_PAD), BF16),
            jax.ShapeDtypeStruct((MLA_HEADS, V_PAD, t), BF16),
        ],
        compiler_params=pltpu.CompilerParams(
            dimension_semantics=("parallel",), vmem_limit_bytes=VMEM_LIMIT),
        name="proj",
    )(x, pre_g, w_in, q_g, wq_t, wqsw_t, kv_g, w_k, wv_t, gm_g, ws_cat,
      g_bias, g_mask, k_tab, q_cos, q_sin)


def _mix_body(x_ref, p_ref, gb_ref, ygm_ref, qt_ref, k_ref, vt_ref,
              cw_ref, cb_ref, wout_ref, post_ref, o_ref, ot_ref):
    tq = x_ref.shape[0]
    seq = p_ref.shape[0]
    i = pl.program_id(1)
    r0 = pl.multiple_of(i * tq, tq)

    cur = p_ref[pl.ds(r0, tq), :]
    lo = pl.multiple_of(jnp.maximum(r0 - 8, 0), 8)
    hi = pl.multiple_of(jnp.minimum(r0 + tq, seq - 8), 8)
    before = p_ref[pl.ds(lo, 8), :][7:8, :]
    after = p_ref[pl.ds(hi, 8), :][0:1, :]
    before = jnp.where(i > 0, before, 0.0)
    after = jnp.where(i < pl.num_programs(1) - 1, after, 0.0)
    row = lax.broadcasted_iota(jnp.int32, (tq, CONV_W), 0)
    prev = jnp.where(row == 0, before, pltpu.roll(cur, 1, 0))
    nxt = jnp.where(row == tq - 1, after, pltpu.roll(cur, tq - 1, 0))
    cw = cw_ref[...]
    conv = prev * cw[0:1, :] + cur * cw[1:2, :] + nxt * cw[2:3, :]
    y_conv = gb_ref[...] * (conv + cb_ref[...])

    kc = min(KEY_CHUNK, seq)
    nc = seq // kc
    units = [(h, c) for h in range(MLA_HEADS) for c in range(nc)]

    def scores(unit):
        h, c = unit
        return jnp.dot(k_ref[h, c * kc:(c + 1) * kc, :], qt_ref[h],
                       preferred_element_type=F32)

    pending = [scores(u) for u in units[:SCORE_LOOKAHEAD]]
    m = acc = None
    for idx, (h, c) in enumerate(units):
        if idx + SCORE_LOOKAHEAD < len(units):
            pending.append(scores(units[idx + SCORE_LOOKAHEAD]))
        s = pending.pop(0)
        m_c = jnp.max(s, axis=0, keepdims=True)
        m_new = m_c if c == 0 else jnp.maximum(m, m_c)
        e = jnp.exp2(s - m_new)
        pv = jnp.dot(vt_ref[h, :, c * kc:(c + 1) * kc], e.astype(BF16),
                     preferred_element_type=F32)
        acc = pv if c == 0 else jnp.exp2(m - m_new) * acc + pv
        m = m_new
        if c == nc - 1:
            ot_ref[h * V_DIM:(h + 1) * V_DIM, :] = acc[0:V_DIM] / acc[V_DIM:V_DIM + 1]
    y_mla = ot_ref[...].T

    ycat = jnp.concatenate(
        [y_conv.astype(BF16), y_mla.astype(BF16), ygm_ref[...]], axis=1)
    y = jnp.dot(ycat, wout_ref[...], preferred_element_type=F32)
    o_ref[...] = x_ref[...] + _rms(y, post_ref[...])


def _mix(x, batch, seq, p, gb, ygm, qt, k, vt, conv_w, conv_b, w_out, post_g):
    t = x.shape[0]
    tq = min(TQ_MIX, seq)
    nq = seq // tq
    tok = lambda b, i: (b * nq + i, 0)
    return pl.pallas_call(
        _mix_body,
        grid=(batch, nq),
        in_specs=[
            pl.BlockSpec((tq, D_MODEL), tok),
            pl.BlockSpec((seq, CONV_W), lambda b, i: (b, 0)),
            pl.BlockSpec((tq, CONV_W), tok),
            pl.BlockSpec((tq, GMLP_W), tok),
            pl.BlockSpec((MLA_HEADS, HEAD_PAD, tq), lambda b, i: (0, 0, b * nq + i)),
            pl.BlockSpec((MLA_HEADS, seq, HEAD_PAD), lambda b, i: (0, b, 0)),
            pl.BlockSpec((MLA_HEADS, V_PAD, seq), lambda b, i: (0, 0, b)),
            _const_spec((8, CONV_W)),
            _const_spec((1, CONV_W)),
            _const_spec((D_MODEL, D_MODEL)),
            _const_spec((1, D_MODEL)),
        ],
        out_specs=pl.BlockSpec((tq, D_MODEL), tok),
        out_shape=jax.ShapeDtypeStruct((t, D_MODEL), F32),
        scratch_shapes=[pltpu.VMEM((MLA_W, tq), F32)],
        compiler_params=pltpu.CompilerParams(
            dimension_semantics=("parallel", "arbitrary"),
            vmem_limit_bytes=VMEM_LIMIT),
        name="mix",
    )(x, p, gb, ygm, qt, k, vt, conv_w, conv_b, w_out, post_g)


def _rope_tables(seq):
    pos = jnp.arange(seq, dtype=F32)
    inv = 1.0 / (ROPE_THETA ** (jnp.arange(0, ROPE_DIM, 2, dtype=F32) / ROPE_DIM))
    ang = pos[:, None] * inv[None, :]
    cos, sin = jnp.cos(ang), jnp.sin(ang)
    cc = jnp.concatenate([cos, cos], axis=1)
    ss = jnp.concatenate([-sin, sin], axis=1)
    k_tab = jnp.concatenate(
        [cc, ss, jnp.zeros((seq, LANES - 2 * ROPE_DIM), F32)], axis=1)
    return k_tab, cc.T, ss.T


def _half_swap(w, axis):
    a, b = jnp.split(w, 2, axis=axis)
    return jnp.concatenate([b, a], axis=axis)


def _prep_layer(w_in, w_uq, w_ukv, gmlp_ws, gmlp_b, conv_w):
    s = np.cumsum([CONV_W, CONV_W, CONV_W, Q_RANK, KV_RANK, ROPE_DIM, GMLP_W]).tolist()
    xc, gb, gc, cq, ckv, kr, zu, zv = jnp.split(w_in, s, axis=1)
    pad = jnp.zeros((D_MODEL, N_IN_PAD - N_IN - ROPE_DIM), F32)
    w_in_p = jnp.concatenate(
        [xc, gb, gc, cq, zu, zv, ckv, kr, _half_swap(kr, 1), pad], axis=1).astype(BF16)

    wq = w_uq.reshape(Q_RANK, MLA_HEADS, QK_DIM)
    wq_t = w_uq.T.astype(BF16)
    wqsw_t = _half_swap(wq[:, :, NOPE_DIM:], 2).reshape(
        Q_RANK, MLA_HEADS * ROPE_DIM).T.astype(BF16)

    wkv = w_ukv.reshape(KV_RANK, MLA_HEADS, NOPE_DIM + V_DIM)
    wk_nope = jnp.pad(wkv[:, :, :NOPE_DIM], ((0, 0), (0, 0), (0, HEAD_PAD - NOPE_DIM)))
    place = jnp.zeros((LANES, MLA_HEADS, HEAD_PAD), F32)
    j = jnp.arange(ROPE_DIM)
    place = place.at[j, :, NOPE_DIM + j].set(1.0)
    w_k = jnp.concatenate([wk_nope, place], axis=0).reshape(
        2 * LANES, MLA_HEADS * HEAD_PAD).astype(BF16)
    wv_t = wkv[:, :, NOPE_DIM:].reshape(KV_RANK, MLA_W).T.astype(BF16)

    ws_cat = jnp.transpose(gmlp_ws, (1, 0, 2)).reshape(
        CHUNK, GMLP_HEADS * CHUNK).astype(BF16)
    g_bias = jnp.repeat(gmlp_b.T, GMLP_HD, axis=1)
    rg = jnp.arange(GMLP_HEADS * CHUNK)[:, None] // CHUNK
    cg = jnp.arange(GMLP_W)[None, :] // GMLP_HD
    g_mask = (rg == cg).astype(BF16)
    conv_w8 = jnp.pad(conv_w, ((0, 8 - conv_w.shape[0]), (0, 0)))
    return w_in_p, wq_t, wqsw_t, w_k, wv_t, ws_cat, g_bias, g_mask, conv_w8


def kernel(x, ffn1_pre_g, ffn1_w_gu, ffn1_w_down, ffn1_post_g, mix_pre_g, w_in,
           conv_w, conv_b, q_norm_g, w_uq, kv_norm_g, w_ukv, gmlp_norm_g, gmlp_ws,
           gmlp_b, w_out, mix_post_g, ffn2_pre_g, ffn2_w_gu, ffn2_w_down, ffn2_post_g):
    batch, seq, _ = x.shape
    depth = w_in.shape[0]
    k_tab, q_cos, q_sin = _rope_tables(seq)
    row = lambda g: g.reshape(1, -1)
    h = x.reshape(batch * seq, D_MODEL)
    for l in range(depth):
        (w_in_p, wq_t, wqsw_t, w_k, wv_t, ws_cat, g_bias, g_mask, conv_w8) = _prep_layer(
            w_in[l], w_uq[l], w_ukv[l], gmlp_ws[l], gmlp_b[l], conv_w[l])
        h = _ffn(h, row(ffn1_pre_g[l]), ffn1_w_gu[l].astype(BF16),
                 ffn1_w_down[l].astype(BF16), row(ffn1_post_g[l]))
        p, gb, ygm, qt, k, vt = _proj(
            h, seq, row(mix_pre_g[l]), w_in_p, row(q_norm_g[l]), wq_t, wqsw_t,
            row(kv_norm_g[l]), w_k, wv_t, row(gmlp_norm_g[l]), ws_cat, g_bias,
            g_mask, k_tab, q_cos, q_sin)
        h = _mix(h, batch, seq, p, gb, ygm, qt, k, vt, conv_w8, row(conv_b[l]),
                 w_out[l].astype(BF16), row(mix_post_g[l]))
        h = _ffn(h, row(ffn2_pre_g[l]), ffn2_w_gu[l].astype(BF16),
                 ffn2_w_down[l].astype(BF16), row(ffn2_post_g[l]))
    return h.reshape(batch, seq, D_MODEL)
```

```python
import math

import jax
import jax.numpy as jnp
import numpy as np
from jax import lax
from jax.experimental import pallas as pl
from jax.experimental.pallas import tpu as pltpu

F32 = jnp.float32
BF16 = jnp.bfloat16

D_MODEL = 1024
D_FF = 2816
EPS = 1e-6
CONV_W = 256
MLA_HEADS = 8
NOPE_DIM = 64
ROPE_DIM = 32
V_DIM = 64
Q_RANK = 256
KV_RANK = 128
QK_DIM = NOPE_DIM + ROPE_DIM
MLA_W = MLA_HEADS * V_DIM
ROPE_THETA = 10000.0
GMLP_W = 256
GMLP_HEADS = 4
GMLP_HD = 64
CHUNK = 128
N_IN = 3 * CONV_W + Q_RANK + KV_RANK + ROPE_DIM + 2 * GMLP_W

LANES = 128
HEAD_PAD = LANES
V_PAD = V_DIM + 16
C_XC, C_GB, C_GC, C_CQ, C_ZU, C_ZV, C_CKV, C_KR = 0, 256, 512, 768, 1024, 1280, 1536, 1664
N_IN_PAD = 1792

TM_FFN = 1024
FFN_SUBTILES = 4
TM_PROJ = 1024
PROJ_SUBTILES = 4
TQ_MIX = 512
MIX_SUBTILES = 2
KEY_CHUNK = 256
SCORE_LOOKAHEAD = 6
VMEM_LIMIT = 56 * 1024 * 1024


def _rms(x, g):
    return x * lax.rsqrt(jnp.mean(x * x, axis=-1, keepdims=True) + EPS) * g


def _gelu(x):
    return 0.5 * x * (1.0 + lax.erf(x * math.sqrt(0.5)))


def _const_spec(shape):
    nd = len(shape)
    return pl.BlockSpec(shape, lambda *_: (0,) * nd, pipeline_mode=pl.Buffered(1))


def _ffn_body(x_ref, pre_ref, wgu_ref, wd_ref, post_ref, o_ref):
    tm = x_ref.shape[0]
    r = tm // FFN_SUBTILES
    rows = [slice(i * r, (i + 1) * r) for i in range(FFN_SUBTILES)]
    gus = [jnp.dot(_rms(x_ref[s, :], pre_ref[...]).astype(BF16), wgu_ref[...],
                   preferred_element_type=F32) for s in rows]
    ds = [jnp.dot((jax.nn.silu(gu[:, :D_FF]) * gu[:, D_FF:]).astype(BF16), wd_ref[...],
                  preferred_element_type=F32) for gu in gus]
    for s, d in zip(rows, ds):
        o_ref[s, :] = x_ref[s, :] + 0.5 * _rms(d, post_ref[...])


def _ffn(x, pre_g, w_gu, w_down, post_g):
    t = x.shape[0]
    tm = min(TM_FFN, t)
    return pl.pallas_call(
        _ffn_body,
        grid=(t // tm,),
        in_specs=[
            pl.BlockSpec((tm, D_MODEL), lambda i: (i, 0)),
            _const_spec((1, D_MODEL)),
            _const_spec((D_MODEL, 2 * D_FF)),
            _const_spec((D_FF, D_MODEL)),
            _const_spec((1, D_MODEL)),
        ],
        out_specs=pl.BlockSpec((tm, D_MODEL), lambda i: (i, 0)),
        out_shape=jax.ShapeDtypeStruct((t, D_MODEL), F32),
        compiler_params=pltpu.CompilerParams(
            dimension_semantics=("parallel",), vmem_limit_bytes=VMEM_LIMIT),
        name="ffn",
    )(x, pre_g, w_gu, w_down, post_g)


def _proj_body(x_ref, pre_ref, win_ref, qg_ref, wqt_ref, wqswt_ref, kvg_ref,
               wk_ref, wvt_ref, gg_ref, wscat_ref, gbias_ref, gmask_ref,
               ktab_ref, qcos_ref, qsin_ref,
               p_ref, gb_ref, ygm_ref, qt_ref, k_ref, vt_ref):
    tm = x_ref.shape[0]
    r = tm // PROJ_SUBTILES
    rows = [slice(i * r, (i + 1) * r) for i in range(PROJ_SUBTILES)]
    zs = [jnp.dot(_rms(x_ref[s, :], pre_ref[...]).astype(BF16), win_ref[...],
                  preferred_element_type=F32) for s in rows]
    scale = math.log2(math.e) / math.sqrt(QK_DIM)
    nt = (((1,), (1,)), ((), ()))
    zero_pad = jnp.zeros((HEAD_PAD - QK_DIM, r), BF16)
    ones_rows = (lax.broadcasted_iota(jnp.int32, (V_PAD - V_DIM, r), 0) == 0).astype(BF16)
    wscat = wscat_ref[...]
    gbias = gbias_ref[...]
    gmask = gmask_ref[...] != 0

    for s, z in zip(rows, zs):
        p_ref[s, :] = z[:, C_GC:C_GC + CONV_W] * z[:, C_XC:C_XC + CONV_W]
        gb_ref[s, :] = z[:, C_GB:C_GB + CONV_W]

        cqn = _rms(z[:, C_CQ:C_CQ + Q_RANK], qg_ref[...]).astype(BF16)
        qt = lax.dot_general(wqt_ref[...], cqn, nt, preferred_element_type=F32)
        qswt = lax.dot_general(wqswt_ref[...], cqn, nt, preferred_element_type=F32)
        qcos = qcos_ref[:, s]
        qsin = qsin_ref[:, s]
        for h in range(MLA_HEADS):
            r0 = h * QK_DIM
            nope = qt[r0:r0 + NOPE_DIM] * scale
            rope = (qt[r0 + NOPE_DIM:r0 + QK_DIM] * qcos
                    + qswt[h * ROPE_DIM:(h + 1) * ROPE_DIM] * qsin) * scale
            qt_ref[h, 0:NOPE_DIM, s] = nope.astype(BF16)
            qt_ref[h, NOPE_DIM:QK_DIM, s] = rope.astype(BF16)
            qt_ref[h, QK_DIM:HEAD_PAD, s] = zero_pad

        ckvn = _rms(z[:, C_CKV:C_CKV + KV_RANK], kvg_ref[...]).astype(BF16)
        kr = z[:, C_KR:C_KR + LANES] * ktab_ref[s, :]
        kr = kr + pltpu.roll(kr, LANES - ROPE_DIM, 1)
        kin = jnp.concatenate([ckvn, kr.astype(BF16)], axis=1)
        kall = jnp.dot(kin, wk_ref[...], preferred_element_type=F32).astype(BF16)
        for h in range(MLA_HEADS):
            k_ref[h, s, :] = kall[:, h * HEAD_PAD:(h + 1) * HEAD_PAD]
        vt = lax.dot_general(wvt_ref[...], ckvn, nt, preferred_element_type=F32)
        for h in range(MLA_HEADS):
            vt_ref[h, 0:V_DIM, s] = vt[h * V_DIM:(h + 1) * V_DIM].astype(BF16)
            vt_ref[h, V_DIM:V_PAD, s] = ones_rows

        u = _gelu(z[:, C_ZU:C_ZU + GMLP_W])
        v = _rms(_gelu(z[:, C_ZV:C_ZV + GMLP_W]), gg_ref[...]).astype(BF16)
        for c in range(r // CHUNK):
            vc = v[c * CHUNK:(c + 1) * CHUNK]
            rhs = jnp.where(gmask, jnp.concatenate([vc] * GMLP_HEADS, axis=0),
                            jnp.zeros((), BF16))
            mixed = jnp.dot(wscat, rhs, preferred_element_type=F32) + gbias
            c0 = s.start + c * CHUNK
            ygm_ref[c0:c0 + CHUNK, :] = (u[c * CHUNK:(c + 1) * CHUNK] * mixed).astype(BF16)


def _proj(x, seq, pre_g, w_in, q_g, wq_t, wqsw_t, kv_g, w_k, wv_t, gm_g,
          ws_cat, g_bias, g_mask, k_tab, q_cos, q_sin):
    t = x.shape[0]
    tm = min(TM_PROJ, seq)
    per_seq = seq // tm
    tok = lambda i: (i, 0)
    return pl.pallas_call(
        _proj_body,
        grid=(t // tm,),
        in_specs=[
            pl.BlockSpec((tm, D_MODEL), tok),
            _const_spec((1, D_MODEL)),
            _const_spec((D_MODEL, N_IN_PAD)),
            _const_spec((1, Q_RANK)),
            _const_spec((MLA_HEADS * QK_DIM, Q_RANK)),
            _const_spec((MLA_HEADS * ROPE_DIM, Q_RANK)),
            _const_spec((1, KV_RANK)),
            _const_spec((2 * LANES, MLA_HEADS * HEAD_PAD)),
            _const_spec((MLA_W, KV_RANK)),
            _const_spec((1, GMLP_W)),
            _const_spec((CHUNK, GMLP_HEADS * CHUNK)),
            _const_spec((CHUNK, GMLP_W)),
            _const_spec((GMLP_HEADS * CHUNK, GMLP_W)),
            pl.BlockSpec((tm, LANES), lambda i: (i % per_seq, 0)),
            pl.BlockSpec((ROPE_DIM, tm), lambda i: (0, i % per_seq)),
            pl.BlockSpec((ROPE_DIM, tm), lambda i: (0, i % per_seq)),
        ],
        out_specs=[
            pl.BlockSpec((tm, CONV_W), tok),
            pl.BlockSpec((tm, CONV_W), tok),
            pl.BlockSpec((tm, GMLP_W), tok),
            pl.BlockSpec((MLA_HEADS, HEAD_PAD, tm), lambda i: (0, 0, i)),
            pl.BlockSpec((MLA_HEADS, tm, HEAD_PAD), lambda i: (0, i, 0)),
            pl.BlockSpec((MLA_HEADS, V_PAD, tm), lambda i: (0, 0, i)),
        ],
        out_shape=[
            jax.ShapeDtypeStruct((t, CONV_W), F32),
            jax.ShapeDtypeStruct((t, CONV_W), F32),
            jax.ShapeDtypeStruct((t, GMLP_W), BF16),
            jax.ShapeDtypeStruct((MLA_HEADS, HEAD_PAD, t), BF16),
            jax.ShapeDtypeStruct((MLA_HEADS, t, HEAD_PAD), BF16),
            jax.ShapeDtypeStruct((MLA_HEADS, V_PAD, t), BF16),
        ],
        compiler_params=pltpu.CompilerParams(
            dimension_semantics=("parallel",), vmem_limit_bytes=VMEM_LIMIT),
        name="proj",
    )(x, pre_g, w_in, q_g, wq_t, wqsw_t, kv_g, w_k, wv_t, gm_g, ws_cat,
      g_bias, g_mask, k_tab, q_cos, q_sin)


def _mix_body(x_ref, p_ref, gb_ref, ygm_ref, qt_ref, k_ref, vt_ref,
              cw_ref, cb_ref, wout_ref, post_ref, o_ref, ot_ref):
    tq = x_ref.shape[0]
    seq = p_ref.shape[0]
    r = tq // MIX_SUBTILES
    step = pl.program_id(1)
    last_step = pl.num_programs(1) - 1

    def finish(j):
        rows = slice(j * r, (j + 1) * r)
        r0 = pl.multiple_of(step * tq + j * r, r)
        cur = p_ref[pl.ds(r0, r), :]
        lo = pl.multiple_of(jnp.maximum(r0 - 8, 0), 8)
        hi = pl.multiple_of(jnp.minimum(r0 + r, seq - 8), 8)
        before = p_ref[pl.ds(lo, 8), :][7:8, :]
        after = p_ref[pl.ds(hi, 8), :][0:1, :]
        if j == 0:
            before = jnp.where(step > 0, before, 0.0)
        if j == MIX_SUBTILES - 1:
            after = jnp.where(step < last_step, after, 0.0)
        row = lax.broadcasted_iota(jnp.int32, (r, CONV_W), 0)
        prev = jnp.where(row == 0, before, pltpu.roll(cur, 1, 0))
        nxt = jnp.where(row == r - 1, after, pltpu.roll(cur, r - 1, 0))
        cw = cw_ref[...]
        conv = prev * cw[0:1, :] + cur * cw[1:2, :] + nxt * cw[2:3, :]
        y_conv = gb_ref[rows, :] * (conv + cb_ref[...])
        y_mla = ot_ref[j].T
        ycat = jnp.concatenate(
            [y_conv.astype(BF16), y_mla.astype(BF16), ygm_ref[rows, :]], axis=1)
        y = jnp.dot(ycat, wout_ref[...], preferred_element_type=F32)
        o_ref[rows, :] = x_ref[rows, :] + _rms(y, post_ref[...])

    kc = min(KEY_CHUNK, seq)
    nc = seq // kc
    units = [(j, h, c) for j in range(MIX_SUBTILES)
             for h in range(MLA_HEADS) for c in range(nc)]

    def scores(unit):
        j, h, c = unit
        return jnp.dot(k_ref[h, c * kc:(c + 1) * kc, :], qt_ref[h, :, j * r:(j + 1) * r],
                       preferred_element_type=F32)

    pending = [scores(u) for u in units[:SCORE_LOOKAHEAD]]
    m = acc = None
    for idx, (j, h, c) in enumerate(units):
        if idx + SCORE_LOOKAHEAD < len(units):
            pending.append(scores(units[idx + SCORE_LOOKAHEAD]))
        s = pending.pop(0)
        m_c = jnp.max(s, axis=0, keepdims=True)
        m_new = m_c if c == 0 else jnp.maximum(m, m_c)
        e = jnp.exp2(s - m_new)
        pv = jnp.dot(vt_ref[h, :, c * kc:(c + 1) * kc], e.astype(BF16),
                     preferred_element_type=F32)
        acc = pv if c == 0 else jnp.exp2(m - m_new) * acc + pv
        m = m_new
        if c == nc - 1:
            ot_ref[j, h * V_DIM:(h + 1) * V_DIM, :] = acc[0:V_DIM] / acc[V_DIM:V_DIM + 1]
            if h == MLA_HEADS - 1:
                finish(j)


def _mix(x, batch, seq, p, gb, ygm, qt, k, vt, conv_w, conv_b, w_out, post_g):
    t = x.shape[0]
    tq = min(TQ_MIX, seq)
    nq = seq // tq
    tok = lambda b, i: (b * nq + i, 0)
    return pl.pallas_call(
        _mix_body,
        grid=(batch, nq),
        in_specs=[
            pl.BlockSpec((tq, D_MODEL), tok),
            pl.BlockSpec((seq, CONV_W), lambda b, i: (b, 0)),
            pl.BlockSpec((tq, CONV_W), tok),
            pl.BlockSpec((tq, GMLP_W), tok),
            pl.BlockSpec((MLA_HEADS, HEAD_PAD, tq), lambda b, i: (0, 0, b * nq + i)),
            pl.BlockSpec((MLA_HEADS, seq, HEAD_PAD), lambda b, i: (0, b, 0)),
            pl.BlockSpec((MLA_HEADS, V_PAD, seq), lambda b, i: (0, 0, b)),
            _const_spec((8, CONV_W)),
            _const_spec((1, CONV_W)),
            _const_spec((D_MODEL, D_MODEL)),
            _const_spec((1, D_MODEL)),
        ],
        out_specs=pl.BlockSpec((tq, D_MODEL), tok),
        out_shape=jax.ShapeDtypeStruct((t, D_MODEL), F32),
        scratch_shapes=[pltpu.VMEM((MIX_SUBTILES, MLA_W, tq // MIX_SUBTILES), F32)],
        compiler_params=pltpu.CompilerParams(
            dimension_semantics=("parallel", "arbitrary"),
            vmem_limit_bytes=VMEM_LIMIT),
        name="mix",
    )(x, p, gb, ygm, qt, k, vt, conv_w, conv_b, w_out, post_g)


def _rope_tables(seq):
    pos = jnp.arange(seq, dtype=F32)
    inv = 1.0 / (ROPE_THETA ** (jnp.arange(0, ROPE_DIM, 2, dtype=F32) / ROPE_DIM))
    ang = pos[:, None] * inv[None, :]
    cos, sin = jnp.cos(ang), jnp.sin(ang)
    cc = jnp.concatenate([cos, cos], axis=1)
    ss = jnp.concatenate([-sin, sin], axis=1)
    k_tab = jnp.concatenate(
        [cc, ss, jnp.zeros((seq, LANES - 2 * ROPE_DIM), F32)], axis=1)
    return k_tab, cc.T, ss.T


def _half_swap(w, axis):
    a, b = jnp.split(w, 2, axis=axis)
    return jnp.concatenate([b, a], axis=axis)


def _prep_layer(w_in, w_uq, w_ukv, gmlp_ws, gmlp_b, conv_w):
    s = np.cumsum([CONV_W, CONV_W, CONV_W, Q_RANK, KV_RANK, ROPE_DIM, GMLP_W]).tolist()
    xc, gb, gc, cq, ckv, kr, zu, zv = jnp.split(w_in, s, axis=1)
    pad = jnp.zeros((D_MODEL, N_IN_PAD - N_IN - ROPE_DIM), F32)
    w_in_p = jnp.concatenate(
        [xc, gb, gc, cq, zu, zv, ckv, kr, _half_swap(kr, 1), pad], axis=1).astype(BF16)

    wq = w_uq.reshape(Q_RANK, MLA_HEADS, QK_DIM)
    wq_t = w_uq.T.astype(BF16)
    wqsw_t = _half_swap(wq[:, :, NOPE_DIM:], 2).reshape(
        Q_RANK, MLA_HEADS * ROPE_DIM).T.astype(BF16)

    wkv = w_ukv.reshape(KV_RANK, MLA_HEADS, NOPE_DIM + V_DIM)
    wk_nope = jnp.pad(wkv[:, :, :NOPE_DIM], ((0, 0), (0, 0), (0, HEAD_PAD - NOPE_DIM)))
    place = jnp.zeros((LANES, MLA_HEADS, HEAD_PAD), F32)
    j = jnp.arange(ROPE_DIM)
    place = place.at[j, :, NOPE_DIM + j].set(1.0)
    w_k = jnp.concatenate([wk_nope, place], axis=0).reshape(
        2 * LANES, MLA_HEADS * HEAD_PAD).astype(BF16)
    wv_t = wkv[:, :, NOPE_DIM:].reshape(KV_RANK, MLA_W).T.astype(BF16)

    ws_cat = jnp.transpose(gmlp_ws, (1, 0, 2)).reshape(
        CHUNK, GMLP_HEADS * CHUNK).astype(BF16)
    g_bias = jnp.repeat(gmlp_b.T, GMLP_HD, axis=1)
    rg = jnp.arange(GMLP_HEADS * CHUNK)[:, None] // CHUNK
    cg = jnp.arange(GMLP_W)[None, :] // GMLP_HD
    g_mask = (rg == cg).astype(BF16)
    conv_w8 = jnp.pad(conv_w, ((0, 8 - conv_w.shape[0]), (0, 0)))
    return w_in_p, wq_t, wqsw_t, w_k, wv_t, ws_cat, g_bias, g_mask, conv_w8


def kernel(x, ffn1_pre_g, ffn1_w_gu, ffn1_w_down, ffn1_post_g, mix_pre_g, w_in,
           conv_w, conv_b, q_norm_g, w_uq, kv_norm_g, w_ukv, gmlp_norm_g, gmlp_ws,
           gmlp_b, w_out, mix_post_g, ffn2_pre_g, ffn2_w_gu, ffn2_w_down, ffn2_post_g):
    batch, seq, _ = x.shape
    depth = w_in.shape[0]
    k_tab, q_cos, q_sin = _rope_tables(seq)
    row = lambda g: g.reshape(1, -1)
    h = x.reshape(batch * seq, D_MODEL)
    for l in range(depth):
        (w_in_p, wq_t, wqsw_t, w_k, wv_t, ws_cat, g_bias, g_mask, conv_w8) = _prep_layer(
            w_in[l], w_uq[l], w_ukv[l], gmlp_ws[l], gmlp_b[l], conv_w[l])
        h = _ffn(h, row(ffn1_pre_g[l]), ffn1_w_gu[l].astype(BF16),
                 ffn1_w_down[l].astype(BF16), row(ffn1_post_g[l]))
        p, gb, ygm, qt, k, vt = _proj(
            h, seq, row(mix_pre_g[l]), w_in_p, row(q_norm_g[l]), wq_t, wqsw_t,
            row(kv_norm_g[l]), w_k, wv_t, row(gmlp_norm_g[l]), ws_cat, g_bias,
            g_mask, k_tab, q_cos, q_sin)
        h = _mix(h, batch, seq, p, gb, ygm, qt, k, vt, conv_w8, row(conv_b[l]),
                 w_out[l].astype(BF16), row(mix_post_g[l]))
        h = _ffn(h, row(ffn2_pre_g[l]), ffn2_w_gu[l].astype(BF16),
                 ffn2_w_down[l].astype(BF16), row(ffn2_post_g[l]))
    return h.reshape(batch, seq, D_MODEL)
```

```python
import math

import jax
import jax.numpy as jnp
import numpy as np
from jax import lax
from jax.experimental import pallas as pl
from jax.experimental.pallas import tpu as pltpu

F32 = jnp.float32
BF16 = jnp.bfloat16

D_MODEL = 1024
D_FF = 2816
EPS = 1e-6
CONV_W = 256
MLA_HEADS = 8
NOPE_DIM = 64
ROPE_DIM = 32
V_DIM = 64
Q_RANK = 256
KV_RANK = 128
QK_DIM = NOPE_DIM + ROPE_DIM
MLA_W = MLA_HEADS * V_DIM
ROPE_THETA = 10000.0
GMLP_W = 256
GMLP_HEADS = 4
GMLP_HD = 64
CHUNK = 128
N_IN = 3 * CONV_W + Q_RANK + KV_RANK + ROPE_DIM + 2 * GMLP_W

LANES = 128
HEAD_PAD = LANES
V_PAD = V_DIM + 16
C_XC, C_GB, C_GC, C_CQ, C_ZU, C_ZV, C_CKV, C_KR = 0, 256, 512, 768, 1024, 1280, 1536, 1664
N_IN_PAD = 1792

TM_FFN = 1024
FFN_SUBTILES = 4
TM_PROJ = 1024
PROJ_SUBTILES = 4
TQ_MIX = 512
MIX_SUBTILES = 2
KEY_CHUNK = 256
SCORE_LOOKAHEAD = 6
VMEM_LIMIT = 56 * 1024 * 1024


def _rms(x, g):
    return x * lax.rsqrt(jnp.mean(x * x, axis=-1, keepdims=True) + EPS) * g


def _gelu(x):
    return 0.5 * x * (1.0 + lax.erf(x * math.sqrt(0.5)))


def _const_spec(shape):
    nd = len(shape)
    return pl.BlockSpec(shape, lambda *_: (0,) * nd, pipeline_mode=pl.Buffered(1))


def _layer_spec(shape, layer):
    nd = len(shape)
    return pl.BlockSpec((None,) + tuple(shape), lambda *_: (layer,) + (0,) * nd,
                        pipeline_mode=pl.Buffered(1))


def _ffn_body(x_ref, pre_ref, wgu_ref, wd_ref, post_ref, o_ref):
    tm = x_ref.shape[0]
    r = tm // FFN_SUBTILES
    rows = [slice(i * r, (i + 1) * r) for i in range(FFN_SUBTILES)]
    gus = [jnp.dot(_rms(x_ref[s, :], pre_ref[...]).astype(BF16), wgu_ref[...],
                   preferred_element_type=F32) for s in rows]
    ds = [jnp.dot((jax.nn.silu(gu[:, :D_FF]) * gu[:, D_FF:]).astype(BF16), wd_ref[...],
                  preferred_element_type=F32) for gu in gus]
    for s, d in zip(rows, ds):
        o_ref[s, :] = x_ref[s, :] + 0.5 * _rms(d, post_ref[...])


def _ffn(x, layer, pre_g, w_gu, w_down, post_g):
    t = x.shape[0]
    tm = min(TM_FFN, t)
    return pl.pallas_call(
        _ffn_body,
        grid=(t // tm,),
        in_specs=[
            pl.BlockSpec((tm, D_MODEL), lambda i: (i, 0)),
            _layer_spec((1, D_MODEL), layer),
            _layer_spec((D_MODEL, 2 * D_FF), layer),
            _layer_spec((D_FF, D_MODEL), layer),
            _layer_spec((1, D_MODEL), layer),
        ],
        out_specs=pl.BlockSpec((tm, D_MODEL), lambda i: (i, 0)),
        out_shape=jax.ShapeDtypeStruct((t, D_MODEL), F32),
        compiler_params=pltpu.CompilerParams(
            dimension_semantics=("parallel",), vmem_limit_bytes=VMEM_LIMIT),
        name="ffn",
    )(x, pre_g, w_gu, w_down, post_g)


def _proj_body(x_ref, pre_ref, win_ref, qg_ref, wqt_ref, wqswt_ref, kvg_ref,
               wk_ref, wvt_ref, gg_ref, wscat_ref, gbias_ref, gmask_ref,
               ktab_ref, qcos_ref, qsin_ref,
               p_ref, gb_ref, ygm_ref, qt_ref, k_ref, vt_ref):
    tm = x_ref.shape[0]
    r = tm // PROJ_SUBTILES
    rows = [slice(i * r, (i + 1) * r) for i in range(PROJ_SUBTILES)]
    zs = [jnp.dot(_rms(x_ref[s, :], pre_ref[...]).astype(BF16), win_ref[...],
                  preferred_element_type=F32) for s in rows]
    scale = math.log2(math.e) / math.sqrt(QK_DIM)
    nt = (((1,), (1,)), ((), ()))
    zero_pad = jnp.zeros((HEAD_PAD - QK_DIM, r), BF16)
    ones_rows = (lax.broadcasted_iota(jnp.int32, (V_PAD - V_DIM, r), 0) == 0).astype(BF16)
    wscat = wscat_ref[...]
    gbias = gbias_ref[...]
    gmask = gmask_ref[...] != 0

    for s, z in zip(rows, zs):
        p_ref[s, :] = z[:, C_GC:C_GC + CONV_W] * z[:, C_XC:C_XC + CONV_W]
        gb_ref[s, :] = z[:, C_GB:C_GB + CONV_W]

        cqn = _rms(z[:, C_CQ:C_CQ + Q_RANK], qg_ref[...]).astype(BF16)
        qt = lax.dot_general(wqt_ref[...], cqn, nt, preferred_element_type=F32)
        qswt = lax.dot_general(wqswt_ref[...], cqn, nt, preferred_element_type=F32)
        qcos = qcos_ref[:, s]
        qsin = qsin_ref[:, s]
        for h in range(MLA_HEADS):
            r0 = h * QK_DIM
            nope = qt[r0:r0 + NOPE_DIM] * scale
            rope = (qt[r0 + NOPE_DIM:r0 + QK_DIM] * qcos
                    + qswt[h * ROPE_DIM:(h + 1) * ROPE_DIM] * qsin) * scale
            qt_ref[h, 0:NOPE_DIM, s] = nope.astype(BF16)
            qt_ref[h, NOPE_DIM:QK_DIM, s] = rope.astype(BF16)
            qt_ref[h, QK_DIM:HEAD_PAD, s] = zero_pad

        ckvn = _rms(z[:, C_CKV:C_CKV + KV_RANK], kvg_ref[...]).astype(BF16)
        kr = z[:, C_KR:C_KR + LANES] * ktab_ref[s, :]
        kr = kr + pltpu.roll(kr, LANES - ROPE_DIM, 1)
        kin = jnp.concatenate([ckvn, kr.astype(BF16)], axis=1)
        kall = jnp.dot(kin, wk_ref[...], preferred_element_type=F32).astype(BF16)
        for h in range(MLA_HEADS):
            k_ref[h, s, :] = kall[:, h * HEAD_PAD:(h + 1) * HEAD_PAD]
        vt = lax.dot_general(wvt_ref[...], ckvn, nt, preferred_element_type=F32)
        for h in range(MLA_HEADS):
            vt_ref[h, 0:V_DIM, s] = vt[h * V_DIM:(h + 1) * V_DIM].astype(BF16)
            vt_ref[h, V_DIM:V_PAD, s] = ones_rows

        u = _gelu(z[:, C_ZU:C_ZU + GMLP_W])
        v = _rms(_gelu(z[:, C_ZV:C_ZV + GMLP_W]), gg_ref[...]).astype(BF16)
        for c in range(r // CHUNK):
            vc = v[c * CHUNK:(c + 1) * CHUNK]
            rhs = jnp.where(gmask, jnp.concatenate([vc] * GMLP_HEADS, axis=0),
                            jnp.zeros((), BF16))
            mixed = jnp.dot(wscat, rhs, preferred_element_type=F32) + gbias
            c0 = s.start + c * CHUNK
            ygm_ref[c0:c0 + CHUNK, :] = (u[c * CHUNK:(c + 1) * CHUNK] * mixed).astype(BF16)


def _proj(x, seq, layer, pre_g, w_in, q_g, wq_t, wqsw_t, kv_g, w_k, wv_t, gm_g,
          ws_cat, g_bias, g_mask, k_tab, q_cos, q_sin):
    t = x.shape[0]
    tm = min(TM_PROJ, seq)
    per_seq = seq // tm
    tok = lambda i: (i, 0)
    return pl.pallas_call(
        _proj_body,
        grid=(t // tm,),
        in_specs=[
            pl.BlockSpec((tm, D_MODEL), tok),
            _layer_spec((1, D_MODEL), layer),
            _layer_spec((D_MODEL, N_IN_PAD), layer),
            _layer_spec((1, Q_RANK), layer),
            _layer_spec((MLA_HEADS * QK_DIM, Q_RANK), layer),
            _layer_spec((MLA_HEADS * ROPE_DIM, Q_RANK), layer),
            _layer_spec((1, KV_RANK), layer),
            _layer_spec((2 * LANES, MLA_HEADS * HEAD_PAD), layer),
            _layer_spec((MLA_W, KV_RANK), layer),
            _layer_spec((1, GMLP_W), layer),
            _layer_spec((CHUNK, GMLP_HEADS * CHUNK), layer),
            _layer_spec((CHUNK, GMLP_W), layer),
            _const_spec((GMLP_HEADS * CHUNK, GMLP_W)),
            pl.BlockSpec((tm, LANES), lambda i: (i % per_seq, 0)),
            pl.BlockSpec((ROPE_DIM, tm), lambda i: (0, i % per_seq)),
            pl.BlockSpec((ROPE_DIM, tm), lambda i: (0, i % per_seq)),
        ],
        out_specs=[
            pl.BlockSpec((tm, CONV_W), tok),
            pl.BlockSpec((tm, CONV_W), tok),
            pl.BlockSpec((tm, GMLP_W), tok),
            pl.BlockSpec((MLA_HEADS, HEAD_PAD, tm), lambda i: (0, 0, i)),
            pl.BlockSpec((MLA_HEADS, tm, HEAD_PAD), lambda i: (0, i, 0)),
            pl.BlockSpec((MLA_HEADS, V_PAD, tm), lambda i: (0, 0, i)),
        ],
        out_shape=[
            jax.ShapeDtypeStruct((t, CONV_W), F32),
            jax.ShapeDtypeStruct((t, CONV_W), F32),
            jax.ShapeDtypeStruct((t, GMLP_W), BF16),
            jax.ShapeDtypeStruct((MLA_HEADS, HEAD_PAD, t), BF16),
            jax.ShapeDtypeStruct((MLA_HEADS, t, HEAD_PAD), BF16),
            jax.ShapeDtypeStruct((MLA_HEADS, V_PAD, t), BF16),
        ],
        compiler_params=pltpu.CompilerParams(
            dimension_semantics=("parallel",), vmem_limit_bytes=VMEM_LIMIT),
        name="proj",
    )(x, pre_g, w_in, q_g, wq_t, wqsw_t, kv_g, w_k, wv_t, gm_g, ws_cat,
      g_bias, g_mask, k_tab, q_cos, q_sin)


def _mix_body(x_ref, p_ref, gb_ref, ygm_ref, qt_ref, k_ref, vt_ref,
              cw_ref, cb_ref, wout_ref, post_ref, o_ref, ot_ref):
    tq = x_ref.shape[0]
    seq = p_ref.shape[0]
    r = tq // MIX_SUBTILES
    step = pl.program_id(1)
    last_step = pl.num_programs(1) - 1

    def finish(j):
        rows = slice(j * r, (j + 1) * r)
        r0 = pl.multiple_of(step * tq + j * r, r)
        cur = p_ref[pl.ds(r0, r), :]
        lo = pl.multiple_of(jnp.maximum(r0 - 8, 0), 8)
        hi = pl.multiple_of(jnp.minimum(r0 + r, seq - 8), 8)
        before = p_ref[pl.ds(lo, 8), :][7:8, :]
        after = p_ref[pl.ds(hi, 8), :][0:1, :]
        if j == 0:
            before = jnp.where(step > 0, before, 0.0)
        if j == MIX_SUBTILES - 1:
            after = jnp.where(step < last_step, after, 0.0)
        row = lax.broadcasted_iota(jnp.int32, (r, CONV_W), 0)
        prev = jnp.where(row == 0, before, pltpu.roll(cur, 1, 0))
        nxt = jnp.where(row == r - 1, after, pltpu.roll(cur, r - 1, 0))
        cw = cw_ref[...]
        conv = prev * cw[0:1, :] + cur * cw[1:2, :] + nxt * cw[2:3, :]
        y_conv = gb_ref[rows, :] * (conv + cb_ref[...])
        y_mla = ot_ref[j].T
        ycat = jnp.concatenate(
            [y_conv.astype(BF16), y_mla.astype(BF16), ygm_ref[rows, :]], axis=1)
        y = jnp.dot(ycat, wout_ref[...], preferred_element_type=F32)
        o_ref[rows, :] = x_ref[rows, :] + _rms(y, post_ref[...])

    kc = min(KEY_CHUNK, seq)
    nc = seq // kc
    units = [(j, h, c) for j in range(MIX_SUBTILES)
             for h in range(MLA_HEADS) for c in range(nc)]

    def scores(unit):
        j, h, c = unit
        return jnp.dot(k_ref[h, c * kc:(c + 1) * kc, :], qt_ref[h, :, j * r:(j + 1) * r],
                       preferred_element_type=F32)

    pending = [scores(u) for u in units[:SCORE_LOOKAHEAD]]
    m = acc = None
    for idx, (j, h, c) in enumerate(units):
        if idx + SCORE_LOOKAHEAD < len(units):
            pending.append(scores(units[idx + SCORE_LOOKAHEAD]))
        s = pending.pop(0)
        m_c = jnp.max(s, axis=0, keepdims=True)
        m_new = m_c if c == 0 else jnp.maximum(m, m_c)
        e = jnp.exp2(s - m_new)
        pv = jnp.dot(vt_ref[h, :, c * kc:(c + 1) * kc], e.astype(BF16),
                     preferred_element_type=F32)
        acc = pv if c == 0 else jnp.exp2(m - m_new) * acc + pv
        m = m_new
        if c == nc - 1:
            ot_ref[j, h * V_DIM:(h + 1) * V_DIM, :] = acc[0:V_DIM] / acc[V_DIM:V_DIM + 1]
            if h == MLA_HEADS - 1:
                finish(j)


def _mix(x, batch, seq, layer, p, gb, ygm, qt, k, vt, conv_w, conv_b, w_out, post_g):
    t = x.shape[0]
    tq = min(TQ_MIX, seq)
    nq = seq // tq
    tok = lambda b, i: (b * nq + i, 0)
    return pl.pallas_call(
        _mix_body,
        grid=(batch, nq),
        in_specs=[
            pl.BlockSpec((tq, D_MODEL), tok),
            pl.BlockSpec((seq, CONV_W), lambda b, i: (b, 0)),
            pl.BlockSpec((tq, CONV_W), tok),
            pl.BlockSpec((tq, GMLP_W), tok),
            pl.BlockSpec((MLA_HEADS, HEAD_PAD, tq), lambda b, i: (0, 0, b * nq + i)),
            pl.BlockSpec((MLA_HEADS, seq, HEAD_PAD), lambda b, i: (0, b, 0)),
            pl.BlockSpec((MLA_HEADS, V_PAD, seq), lambda b, i: (0, 0, b)),
            _layer_spec((8, CONV_W), layer),
            _layer_spec((1, CONV_W), layer),
            _layer_spec((D_MODEL, D_MODEL), layer),
            _layer_spec((1, D_MODEL), layer),
        ],
        out_specs=pl.BlockSpec((tq, D_MODEL), tok),
        out_shape=jax.ShapeDtypeStruct((t, D_MODEL), F32),
        scratch_shapes=[pltpu.VMEM((MIX_SUBTILES, MLA_W, tq // MIX_SUBTILES), F32)],
        compiler_params=pltpu.CompilerParams(
            dimension_semantics=("parallel", "arbitrary"),
            vmem_limit_bytes=VMEM_LIMIT),
        name="mix",
    )(x, p, gb, ygm, qt, k, vt, conv_w, conv_b, w_out, post_g)


def _rope_tables(seq):
    pos = jnp.arange(seq, dtype=F32)
    inv = 1.0 / (ROPE_THETA ** (jnp.arange(0, ROPE_DIM, 2, dtype=F32) / ROPE_DIM))
    ang = pos[:, None] * inv[None, :]
    cos, sin = jnp.cos(ang), jnp.sin(ang)
    cc = jnp.concatenate([cos, cos], axis=1)
    ss = jnp.concatenate([-sin, sin], axis=1)
    k_tab = jnp.concatenate(
        [cc, ss, jnp.zeros((seq, LANES - 2 * ROPE_DIM), F32)], axis=1)
    return k_tab, cc.T, ss.T


def _half_swap(w, axis):
    a, b = jnp.split(w, 2, axis=axis)
    return jnp.concatenate([b, a], axis=axis)


def _prep_params(w_in, w_uq, w_ukv, gmlp_ws, gmlp_b, conv_w):
    depth = w_in.shape[0]
    s = np.cumsum([CONV_W, CONV_W, CONV_W, Q_RANK, KV_RANK, ROPE_DIM, GMLP_W]).tolist()
    xc, gb, gc, cq, ckv, kr, zu, zv = jnp.split(w_in, s, axis=2)
    pad = jnp.zeros((depth, D_MODEL, N_IN_PAD - N_IN - ROPE_DIM), F32)
    w_in_p = jnp.concatenate(
        [xc, gb, gc, cq, zu, zv, ckv, kr, _half_swap(kr, 2), pad], axis=2).astype(BF16)

    wq = w_uq.reshape(depth, Q_RANK, MLA_HEADS, QK_DIM)
    wq_t = jnp.swapaxes(w_uq, 1, 2).astype(BF16)
    wqsw_t = jnp.swapaxes(_half_swap(wq[..., NOPE_DIM:], 3).reshape(
        depth, Q_RANK, MLA_HEADS * ROPE_DIM), 1, 2).astype(BF16)

    wkv = w_ukv.reshape(depth, KV_RANK, MLA_HEADS, NOPE_DIM + V_DIM)
    wk_nope = jnp.pad(wkv[..., :NOPE_DIM],
                      ((0, 0), (0, 0), (0, 0), (0, HEAD_PAD - NOPE_DIM)))
    place = np.zeros((LANES, MLA_HEADS, HEAD_PAD), np.float32)
    j = np.arange(ROPE_DIM)
    place[j, :, NOPE_DIM + j] = 1.0
    place = jnp.broadcast_to(jnp.asarray(place), (depth,) + place.shape)
    w_k = jnp.concatenate([wk_nope, place], axis=1).reshape(
        depth, 2 * LANES, MLA_HEADS * HEAD_PAD).astype(BF16)
    wv_t = jnp.swapaxes(wkv[..., NOPE_DIM:].reshape(depth, KV_RANK, MLA_W),
                        1, 2).astype(BF16)

    ws_cat = jnp.transpose(gmlp_ws, (0, 2, 1, 3)).reshape(
        depth, CHUNK, GMLP_HEADS * CHUNK).astype(BF16)
    g_bias = jnp.repeat(jnp.swapaxes(gmlp_b, 1, 2), GMLP_HD, axis=2)
    rg = np.arange(GMLP_HEADS * CHUNK)[:, None] // CHUNK
    cg = np.arange(GMLP_W)[None, :] // GMLP_HD
    g_mask = jnp.asarray(rg == cg, BF16)
    conv_w8 = jnp.pad(conv_w, ((0, 0), (0, 8 - conv_w.shape[1]), (0, 0)))
    return w_in_p, wq_t, wqsw_t, w_k, wv_t, ws_cat, g_bias, g_mask, conv_w8


def kernel(x, ffn1_pre_g, ffn1_w_gu, ffn1_w_down, ffn1_post_g, mix_pre_g, w_in,
           conv_w, conv_b, q_norm_g, w_uq, kv_norm_g, w_ukv, gmlp_norm_g, gmlp_ws,
           gmlp_b, w_out, mix_post_g, ffn2_pre_g, ffn2_w_gu, ffn2_w_down, ffn2_post_g):
    batch, seq, _ = x.shape
    depth = w_in.shape[0]
    k_tab, q_cos, q_sin = _rope_tables(seq)
    rows = lambda g: g[:, None, :]
    (w_in_p, wq_t, wqsw_t, w_k, wv_t, ws_cat, g_bias, g_mask, conv_w8) = _prep_params(
        w_in, w_uq, w_ukv, gmlp_ws, gmlp_b, conv_w)
    ffn1 = (rows(ffn1_pre_g), ffn1_w_gu.astype(BF16), ffn1_w_down.astype(BF16),
            rows(ffn1_post_g))
    ffn2 = (rows(ffn2_pre_g), ffn2_w_gu.astype(BF16), ffn2_w_down.astype(BF16),
            rows(ffn2_post_g))
    w_out_b = w_out.astype(BF16)
    h = x.reshape(batch * seq, D_MODEL)
    for l in range(depth):
        h = _ffn(h, l, *ffn1)
        p, gb, ygm, qt, k, vt = _proj(
            h, seq, l, rows(mix_pre_g), w_in_p, rows(q_norm_g), wq_t, wqsw_t,
            rows(kv_norm_g), w_k, wv_t, rows(gmlp_norm_g), ws_cat, g_bias,
            g_mask, k_tab, q_cos, q_sin)
        h = _mix(h, batch, seq, l, p, gb, ygm, qt, k, vt, conv_w8, rows(conv_b),
                 w_out_b, rows(mix_post_g))
        h = _ffn(h, l, *ffn2)
    return h.reshape(batch, seq, D_MODEL)
```

```python
import math

import jax
import jax.numpy as jnp
import numpy as np
from jax import lax
from jax.experimental import pallas as pl
from jax.experimental.pallas import tpu as pltpu

F32 = jnp.float32
BF16 = jnp.bfloat16

D_MODEL = 1024
D_FF = 2816
EPS = 1e-6
CONV_W = 256
MLA_HEADS = 8
NOPE_DIM = 64
ROPE_DIM = 32
V_DIM = 64
Q_RANK = 256
KV_RANK = 128
QK_DIM = NOPE_DIM + ROPE_DIM
MLA_W = MLA_HEADS * V_DIM
ROPE_THETA = 10000.0
GMLP_W = 256
GMLP_HEADS = 4
GMLP_HD = 64
CHUNK = 128
N_IN = 3 * CONV_W + Q_RANK + KV_RANK + ROPE_DIM + 2 * GMLP_W

LANES = 128
HEAD_PAD = LANES
V_PAD = V_DIM + 16
C_XC, C_GB, C_GC, C_CQ, C_ZU, C_ZV, C_CKV, C_KR = 0, 256, 512, 768, 1024, 1280, 1536, 1664
N_IN_PAD = 1792

TM_FFN = 1024
FFN_SUBTILES = 4
FFN_CAST_STEPS = 16
TM_PROJ = 1024
PROJ_SUBTILES = 4
TQ_MIX = 512
MIX_SUBTILES = 2
KEY_CHUNK = 256
SCORE_LOOKAHEAD = 6
VMEM_LIMIT = 56 * 1024 * 1024


def _rms(x, g):
    return x * lax.rsqrt(jnp.mean(x * x, axis=-1, keepdims=True) + EPS) * g


def _gelu(x):
    return 0.5 * x * (1.0 + lax.erf(x * math.sqrt(0.5)))


def _const_spec(shape):
    nd = len(shape)
    return pl.BlockSpec(shape, lambda *_: (0,) * nd, pipeline_mode=pl.Buffered(1))


def _layer_spec(shape, layer):
    nd = len(shape)
    return pl.BlockSpec((None,) + tuple(shape), lambda *_: (layer,) + (0,) * nd,
                        pipeline_mode=pl.Buffered(1))


def _ffn_body(x_ref, pre_ref, wgu_ref, wd_ref, post_ref, o_ref, wgu_bf, wd_bf):
    step = pl.program_id(0)

    @pl.when(step < FFN_CAST_STEPS)
    def _():
        gr, dr = wgu_ref.shape[0], wd_ref.shape[0]
        wgu_bf[pl.ds(pl.multiple_of(step * gr, gr), gr), :] = wgu_ref[...].astype(BF16)
        wd_bf[pl.ds(pl.multiple_of(step * dr, dr), dr), :] = wd_ref[...].astype(BF16)

    @pl.when(step >= FFN_CAST_STEPS)
    def _():
        tm = x_ref.shape[0]
        r = tm // FFN_SUBTILES
        rows = [slice(i * r, (i + 1) * r) for i in range(FFN_SUBTILES)]
        gus = [jnp.dot(_rms(x_ref[s, :], pre_ref[...]).astype(BF16), wgu_bf[...],
                       preferred_element_type=F32) for s in rows]
        ds = [jnp.dot((jax.nn.silu(gu[:, :D_FF]) * gu[:, D_FF:]).astype(BF16), wd_bf[...],
                      preferred_element_type=F32) for gu in gus]
        for s, d in zip(rows, ds):
            o_ref[s, :] = x_ref[s, :] + 0.5 * _rms(d, post_ref[...])


def _ffn(x, layer, pre_g, w_gu, w_down, post_g):
    t = x.shape[0]
    tm = min(TM_FFN, t)
    nc = FFN_CAST_STEPS
    tok = lambda i: (jnp.maximum(i - nc, 0), 0)
    chunk = lambda i: (layer, jnp.minimum(i, nc - 1), 0)
    return pl.pallas_call(
        _ffn_body,
        grid=(nc + t // tm,),
        in_specs=[
            pl.BlockSpec((tm, D_MODEL), tok),
            _layer_spec((1, D_MODEL), layer),
            pl.BlockSpec((None, D_MODEL // nc, 2 * D_FF), chunk),
            pl.BlockSpec((None, D_FF // nc, D_MODEL), chunk),
            _layer_spec((1, D_MODEL), layer),
        ],
        out_specs=pl.BlockSpec((tm, D_MODEL), tok),
        out_shape=jax.ShapeDtypeStruct((t, D_MODEL), F32),
        scratch_shapes=[pltpu.VMEM((D_MODEL, 2 * D_FF), BF16),
                        pltpu.VMEM((D_FF, D_MODEL), BF16)],
        compiler_params=pltpu.CompilerParams(
            dimension_semantics=("arbitrary",), vmem_limit_bytes=VMEM_LIMIT),
        name="ffn",
    )(x, pre_g, w_gu, w_down, post_g)


def _proj_body(x_ref, pre_ref, win_ref, qg_ref, wqt_ref, wqswt_ref, kvg_ref,
               wk_ref, wvt_ref, gg_ref, wscat_ref, gbias_ref, gmask_ref,
               ktab_ref, qcos_ref, qsin_ref,
               p_ref, gb_ref, ygm_ref, qt_ref, k_ref, vt_ref):
    tm = x_ref.shape[0]
    r = tm // PROJ_SUBTILES
    rows = [slice(i * r, (i + 1) * r) for i in range(PROJ_SUBTILES)]
    zs = [jnp.dot(_rms(x_ref[s, :], pre_ref[...]).astype(BF16), win_ref[...],
                  preferred_element_type=F32) for s in rows]
    scale = math.log2(math.e) / math.sqrt(QK_DIM)
    nt = (((1,), (1,)), ((), ()))
    zero_pad = jnp.zeros((HEAD_PAD - QK_DIM, r), BF16)
    ones_rows = (lax.broadcasted_iota(jnp.int32, (V_PAD - V_DIM, r), 0) == 0).astype(BF16)
    wscat = wscat_ref[...]
    gbias = gbias_ref[...]
    gmask = gmask_ref[...] != 0

    for s, z in zip(rows, zs):
        p_ref[s, :] = z[:, C_GC:C_GC + CONV_W] * z[:, C_XC:C_XC + CONV_W]
        gb_ref[s, :] = z[:, C_GB:C_GB + CONV_W]

        cqn = _rms(z[:, C_CQ:C_CQ + Q_RANK], qg_ref[...]).astype(BF16)
        qt = lax.dot_general(wqt_ref[...], cqn, nt, preferred_element_type=F32)
        qswt = lax.dot_general(wqswt_ref[...], cqn, nt, preferred_element_type=F32)
        qcos = qcos_ref[:, s]
        qsin = qsin_ref[:, s]
        for h in range(MLA_HEADS):
            r0 = h * QK_DIM
            nope = qt[r0:r0 + NOPE_DIM] * scale
            rope = (qt[r0 + NOPE_DIM:r0 + QK_DIM] * qcos
                    + qswt[h * ROPE_DIM:(h + 1) * ROPE_DIM] * qsin) * scale
            qt_ref[h, 0:NOPE_DIM, s] = nope.astype(BF16)
            qt_ref[h, NOPE_DIM:QK_DIM, s] = rope.astype(BF16)
            qt_ref[h, QK_DIM:HEAD_PAD, s] = zero_pad

        ckvn = _rms(z[:, C_CKV:C_CKV + KV_RANK], kvg_ref[...]).astype(BF16)
        kr = z[:, C_KR:C_KR + LANES] * ktab_ref[s, :]
        kr = kr + pltpu.roll(kr, LANES - ROPE_DIM, 1)
        kin = jnp.concatenate([ckvn, kr.astype(BF16)], axis=1)
        kall = jnp.dot(kin, wk_ref[...], preferred_element_type=F32).astype(BF16)
        for h in range(MLA_HEADS):
            k_ref[h, s, :] = kall[:, h * HEAD_PAD:(h + 1) * HEAD_PAD]
        vt = lax.dot_general(wvt_ref[...], ckvn, nt, preferred_element_type=F32)
        for h in range(MLA_HEADS):
            vt_ref[h, 0:V_DIM, s] = vt[h * V_DIM:(h + 1) * V_DIM].astype(BF16)
            vt_ref[h, V_DIM:V_PAD, s] = ones_rows

        u = _gelu(z[:, C_ZU:C_ZU + GMLP_W])
        v = _rms(_gelu(z[:, C_ZV:C_ZV + GMLP_W]), gg_ref[...]).astype(BF16)
        for c in range(r // CHUNK):
            vc = v[c * CHUNK:(c + 1) * CHUNK]
            rhs = jnp.where(gmask, jnp.concatenate([vc] * GMLP_HEADS, axis=0),
                            jnp.zeros((), BF16))
            mixed = jnp.dot(wscat, rhs, preferred_element_type=F32) + gbias
            c0 = s.start + c * CHUNK
            ygm_ref[c0:c0 + CHUNK, :] = (u[c * CHUNK:(c + 1) * CHUNK] * mixed).astype(BF16)


def _proj(x, seq, layer, pre_g, w_in, q_g, wq_t, wqsw_t, kv_g, w_k, wv_t, gm_g,
          ws_cat, g_bias, g_mask, k_tab, q_cos, q_sin):
    t = x.shape[0]
    tm = min(TM_PROJ, seq)
    per_seq = seq // tm
    tok = lambda i: (i, 0)
    return pl.pallas_call(
        _proj_body,
        grid=(t // tm,),
        in_specs=[
            pl.BlockSpec((tm, D_MODEL), tok),
            _layer_spec((1, D_MODEL), layer),
            _layer_spec((D_MODEL, N_IN_PAD), layer),
            _layer_spec((1, Q_RANK), layer),
            _layer_spec((MLA_HEADS * QK_DIM, Q_RANK), layer),
            _layer_spec((MLA_HEADS * ROPE_DIM, Q_RANK), layer),
            _layer_spec((1, KV_RANK), layer),
            _layer_spec((2 * LANES, MLA_HEADS * HEAD_PAD), layer),
            _layer_spec((MLA_W, KV_RANK), layer),
            _layer_spec((1, GMLP_W), layer),
            _layer_spec((CHUNK, GMLP_HEADS * CHUNK), layer),
            _layer_spec((CHUNK, GMLP_W), layer),
            _const_spec((GMLP_HEADS * CHUNK, GMLP_W)),
            pl.BlockSpec((tm, LANES), lambda i: (i % per_seq, 0)),
            pl.BlockSpec((ROPE_DIM, tm), lambda i: (0, i % per_seq)),
            pl.BlockSpec((ROPE_DIM, tm), lambda i: (0, i % per_seq)),
        ],
        out_specs=[
            pl.BlockSpec((tm, CONV_W), tok),
            pl.BlockSpec((tm, CONV_W), tok),
            pl.BlockSpec((tm, GMLP_W), tok),
            pl.BlockSpec((MLA_HEADS, HEAD_PAD, tm), lambda i: (0, 0, i)),
            pl.BlockSpec((MLA_HEADS, tm, HEAD_PAD), lambda i: (0, i, 0)),
            pl.BlockSpec((MLA_HEADS, V_PAD, tm), lambda i: (0, 0, i)),
        ],
        out_shape=[
            jax.ShapeDtypeStruct((t, CONV_W), F32),
            jax.ShapeDtypeStruct((t, CONV_W), F32),
            jax.ShapeDtypeStruct((t, GMLP_W), BF16),
            jax.ShapeDtypeStruct((MLA_HEADS, HEAD_PAD, t), BF16),
            jax.ShapeDtypeStruct((MLA_HEADS, t, HEAD_PAD), BF16),
            jax.ShapeDtypeStruct((MLA_HEADS, V_PAD, t), BF16),
        ],
        compiler_params=pltpu.CompilerParams(
            dimension_semantics=("parallel",), vmem_limit_bytes=VMEM_LIMIT),
        name="proj",
    )(x, pre_g, w_in, q_g, wq_t, wqsw_t, kv_g, w_k, wv_t, gm_g, ws_cat,
      g_bias, g_mask, k_tab, q_cos, q_sin)


def _mix_body(x_ref, p_ref, gb_ref, ygm_ref, qt_ref, k_ref, vt_ref,
              cw_ref, cb_ref, wout_ref, post_ref, o_ref, ot_ref):
    tq = x_ref.shape[0]
    seq = p_ref.shape[0]
    r = tq // MIX_SUBTILES
    step = pl.program_id(1)
    last_step = pl.num_programs(1) - 1

    def finish(j):
        rows = slice(j * r, (j + 1) * r)
        r0 = pl.multiple_of(step * tq + j * r, r)
        cur = p_ref[pl.ds(r0, r), :]
        lo = pl.multiple_of(jnp.maximum(r0 - 8, 0), 8)
        hi = pl.multiple_of(jnp.minimum(r0 + r, seq - 8), 8)
        before = p_ref[pl.ds(lo, 8), :][7:8, :]
        after = p_ref[pl.ds(hi, 8), :][0:1, :]
        if j == 0:
            before = jnp.where(step > 0, before, 0.0)
        if j == MIX_SUBTILES - 1:
            after = jnp.where(step < last_step, after, 0.0)
        row = lax.broadcasted_iota(jnp.int32, (r, CONV_W), 0)
        prev = jnp.where(row == 0, before, pltpu.roll(cur, 1, 0))
        nxt = jnp.where(row == r - 1, after, pltpu.roll(cur, r - 1, 0))
        cw = cw_ref[...]
        conv = prev * cw[0:1, :] + cur * cw[1:2, :] + nxt * cw[2:3, :]
        y_conv = gb_ref[rows, :] * (conv + cb_ref[...])
        y_mla = ot_ref[j].T
        ycat = jnp.concatenate(
            [y_conv.astype(BF16), y_mla.astype(BF16), ygm_ref[rows, :]], axis=1)
        y = jnp.dot(ycat, wout_ref[...], preferred_element_type=F32)
        o_ref[rows, :] = x_ref[rows, :] + _rms(y, post_ref[...])

    kc = min(KEY_CHUNK, seq)
    nc = seq // kc
    units = [(j, h, c) for j in range(MIX_SUBTILES)
             for h in range(MLA_HEADS) for c in range(nc)]

    def scores(unit):
        j, h, c = unit
        return jnp.dot(k_ref[h, c * kc:(c + 1) * kc, :], qt_ref[h, :, j * r:(j + 1) * r],
                       preferred_element_type=F32)

    pending = [scores(u) for u in units[:SCORE_LOOKAHEAD]]
    m = acc = None
    for idx, (j, h, c) in enumerate(units):
        if idx + SCORE_LOOKAHEAD < len(units):
            pending.append(scores(units[idx + SCORE_LOOKAHEAD]))
        s = pending.pop(0)
        m_c = jnp.max(s, axis=0, keepdims=True)
        m_new = m_c if c == 0 else jnp.maximum(m, m_c)
        e = jnp.exp2(s - m_new)
        pv = jnp.dot(vt_ref[h, :, c * kc:(c + 1) * kc], e.astype(BF16),
                     preferred_element_type=F32)
        acc = pv if c == 0 else jnp.exp2(m - m_new) * acc + pv
        m = m_new
        if c == nc - 1:
            ot_ref[j, h * V_DIM:(h + 1) * V_DIM, :] = acc[0:V_DIM] / acc[V_DIM:V_DIM + 1]
            if h == MLA_HEADS - 1:
                finish(j)


def _mix(x, batch, seq, layer, p, gb, ygm, qt, k, vt, conv_w, conv_b, w_out, post_g):
    t = x.shape[0]
    tq = min(TQ_MIX, seq)
    nq = seq // tq
    tok = lambda b, i: (b * nq + i, 0)
    return pl.pallas_call(
        _mix_body,
        grid=(batch, nq),
        in_specs=[
            pl.BlockSpec((tq, D_MODEL), tok),
            pl.BlockSpec((seq, CONV_W), lambda b, i: (b, 0)),
            pl.BlockSpec((tq, CONV_W), tok),
            pl.BlockSpec((tq, GMLP_W), tok),
            pl.BlockSpec((MLA_HEADS, HEAD_PAD, tq), lambda b, i: (0, 0, b * nq + i)),
            pl.BlockSpec((MLA_HEADS, seq, HEAD_PAD), lambda b, i: (0, b, 0)),
            pl.BlockSpec((MLA_HEADS, V_PAD, seq), lambda b, i: (0, 0, b)),
            _layer_spec((8, CONV_W), layer),
            _layer_spec((1, CONV_W), layer),
            _layer_spec((D_MODEL, D_MODEL), layer),
            _layer_spec((1, D_MODEL), layer),
        ],
        out_specs=pl.BlockSpec((tq, D_MODEL), tok),
        out_shape=jax.ShapeDtypeStruct((t, D_MODEL), F32),
        scratch_shapes=[pltpu.VMEM((MIX_SUBTILES, MLA_W, tq // MIX_SUBTILES), F32)],
        compiler_params=pltpu.CompilerParams(
            dimension_semantics=("parallel", "arbitrary"),
            vmem_limit_bytes=VMEM_LIMIT),
        name="mix",
    )(x, p, gb, ygm, qt, k, vt, conv_w, conv_b, w_out, post_g)


def _rope_tables(seq):
    pos = jnp.arange(seq, dtype=F32)
    inv = 1.0 / (ROPE_THETA ** (jnp.arange(0, ROPE_DIM, 2, dtype=F32) / ROPE_DIM))
    ang = pos[:, None] * inv[None, :]
    cos, sin = jnp.cos(ang), jnp.sin(ang)
    cc = jnp.concatenate([cos, cos], axis=1)
    ss = jnp.concatenate([-sin, sin], axis=1)
    k_tab = jnp.concatenate(
        [cc, ss, jnp.zeros((seq, LANES - 2 * ROPE_DIM), F32)], axis=1)
    return k_tab, cc.T, ss.T


def _half_swap(w, axis):
    a, b = jnp.split(w, 2, axis=axis)
    return jnp.concatenate([b, a], axis=axis)


def _prep_params(w_in, w_uq, w_ukv, gmlp_ws, gmlp_b, conv_w):
    depth = w_in.shape[0]
    s = np.cumsum([CONV_W, CONV_W, CONV_W, Q_RANK, KV_RANK, ROPE_DIM, GMLP_W]).tolist()
    xc, gb, gc, cq, ckv, kr, zu, zv = jnp.split(w_in, s, axis=2)
    pad = jnp.zeros((depth, D_MODEL, N_IN_PAD - N_IN - ROPE_DIM), F32)
    w_in_p = jnp.concatenate(
        [xc, gb, gc, cq, zu, zv, ckv, kr, _half_swap(kr, 2), pad], axis=2).astype(BF16)

    wq = w_uq.reshape(depth, Q_RANK, MLA_HEADS, QK_DIM)
    wq_t = jnp.swapaxes(w_uq, 1, 2).astype(BF16)
    wqsw_t = jnp.swapaxes(_half_swap(wq[..., NOPE_DIM:], 3).reshape(
        depth, Q_RANK, MLA_HEADS * ROPE_DIM), 1, 2).astype(BF16)

    wkv = w_ukv.reshape(depth, KV_RANK, MLA_HEADS, NOPE_DIM + V_DIM)
    wk_nope = jnp.pad(wkv[..., :NOPE_DIM],
                      ((0, 0), (0, 0), (0, 0), (0, HEAD_PAD - NOPE_DIM)))
    place = np.zeros((LANES, MLA_HEADS, HEAD_PAD), np.float32)
    j = np.arange(ROPE_DIM)
    place[j, :, NOPE_DIM + j] = 1.0
    place = jnp.broadcast_to(jnp.asarray(place), (depth,) + place.shape)
    w_k = jnp.concatenate([wk_nope, place], axis=1).reshape(
        depth, 2 * LANES, MLA_HEADS * HEAD_PAD).astype(BF16)
    wv_t = jnp.swapaxes(wkv[..., NOPE_DIM:].reshape(depth, KV_RANK, MLA_W),
                        1, 2).astype(BF16)

    ws_cat = jnp.transpose(gmlp_ws, (0, 2, 1, 3)).reshape(
        depth, CHUNK, GMLP_HEADS * CHUNK).astype(BF16)
    g_bias = jnp.repeat(jnp.swapaxes(gmlp_b, 1, 2), GMLP_HD, axis=2)
    rg = np.arange(GMLP_HEADS * CHUNK)[:, None] // CHUNK
    cg = np.arange(GMLP_W)[None, :] // GMLP_HD
    g_mask = jnp.asarray(rg == cg, BF16)
    conv_w8 = jnp.pad(conv_w, ((0, 0), (0, 8 - conv_w.shape[1]), (0, 0)))
    return w_in_p, wq_t, wqsw_t, w_k, wv_t, ws_cat, g_bias, g_mask, conv_w8


def kernel(x, ffn1_pre_g, ffn1_w_gu, ffn1_w_down, ffn1_post_g, mix_pre_g, w_in,
           conv_w, conv_b, q_norm_g, w_uq, kv_norm_g, w_ukv, gmlp_norm_g, gmlp_ws,
           gmlp_b, w_out, mix_post_g, ffn2_pre_g, ffn2_w_gu, ffn2_w_down, ffn2_post_g):
    batch, seq, _ = x.shape
    depth = w_in.shape[0]
    k_tab, q_cos, q_sin = _rope_tables(seq)
    rows = lambda g: g[:, None, :]
    (w_in_p, wq_t, wqsw_t, w_k, wv_t, ws_cat, g_bias, g_mask, conv_w8) = _prep_params(
        w_in, w_uq, w_ukv, gmlp_ws, gmlp_b, conv_w)
    ffn1 = (rows(ffn1_pre_g), ffn1_w_gu, ffn1_w_down, rows(ffn1_post_g))
    ffn2 = (rows(ffn2_pre_g), ffn2_w_gu, ffn2_w_down, rows(ffn2_post_g))
    w_out_b = w_out.astype(BF16)
    h = x.reshape(batch * seq, D_MODEL)
    for l in range(depth):
        h = _ffn(h, l, *ffn1)
        p, gb, ygm, qt, k, vt = _proj(
            h, seq, l, rows(mix_pre_g), w_in_p, rows(q_norm_g), wq_t, wqsw_t,
            rows(kv_norm_g), w_k, wv_t, rows(gmlp_norm_g), ws_cat, g_bias,
            g_mask, k_tab, q_cos, q_sin)
        h = _mix(h, batch, seq, l, p, gb, ygm, qt, k, vt, conv_w8, rows(conv_b),
                 w_out_b, rows(mix_post_g))
        h = _ffn(h, l, *ffn2)
    return h.reshape(batch, seq, D_MODEL)
```

```python
import math

import jax
import jax.numpy as jnp
import numpy as np
from jax import lax
from jax.experimental import pallas as pl
from jax.experimental.pallas import tpu as pltpu

F32 = jnp.float32
BF16 = jnp.bfloat16

D_MODEL = 1024
D_FF = 2816
EPS = 1e-6
CONV_W = 256
MLA_HEADS = 8
NOPE_DIM = 64
ROPE_DIM = 32
V_DIM = 64
Q_RANK = 256
KV_RANK = 128
QK_DIM = NOPE_DIM + ROPE_DIM
MLA_W = MLA_HEADS * V_DIM
ROPE_THETA = 10000.0
GMLP_W = 256
GMLP_HEADS = 4
GMLP_HD = 64
CHUNK = 128
N_IN = 3 * CONV_W + Q_RANK + KV_RANK + ROPE_DIM + 2 * GMLP_W

LANES = 128
HEAD_PAD = LANES
V_PAD = V_DIM + 16
C_XC, C_GB, C_GC, C_CQ, C_ZU, C_ZV, C_CKV, C_KR = 0, 256, 512, 768, 1024, 1280, 1536, 1664
N_IN_PAD = 1792

TM_FFN = 1024
FFN_SUBTILES = 4
FFN_CAST_STEPS = 8
TM_PROJ = 1024
PROJ_SUBTILES = 4
TQ_MIX = 1024
MIX_SUBTILES = 4
KEY_CHUNK = 256
SCORE_LOOKAHEAD = 6
VMEM_LIMIT = 56 * 1024 * 1024


def _rms(x, g):
    return x * lax.rsqrt(jnp.mean(x * x, axis=-1, keepdims=True) + EPS) * g


def _gelu(x):
    return 0.5 * x * (1.0 + lax.erf(x * math.sqrt(0.5)))


def _const_spec(shape):
    nd = len(shape)
    return pl.BlockSpec(shape, lambda *_: (0,) * nd, pipeline_mode=pl.Buffered(1))


def _layer_spec(shape, layer):
    nd = len(shape)
    return pl.BlockSpec((None,) + tuple(shape), lambda *_: (layer,) + (0,) * nd,
                        pipeline_mode=pl.Buffered(1))


def _ffn_body(x_ref, pre_ref, wgu_ref, wd_ref, post_ref, o_ref, wgu_bf, wd_bf):
    step = pl.program_id(0)

    @pl.when(step < FFN_CAST_STEPS)
    def _():
        gr, dr = wgu_ref.shape[0], wd_ref.shape[0]
        wgu_bf[pl.ds(pl.multiple_of(step * gr, gr), gr), :] = wgu_ref[...].astype(BF16)
        wd_bf[pl.ds(pl.multiple_of(step * dr, dr), dr), :] = wd_ref[...].astype(BF16)

    @pl.when(step >= FFN_CAST_STEPS)
    def _():
        tm = x_ref.shape[0]
        r = tm // FFN_SUBTILES
        rows = [slice(i * r, (i + 1) * r) for i in range(FFN_SUBTILES)]
        gus = [jnp.dot(_rms(x_ref[s, :], pre_ref[...]).astype(BF16), wgu_bf[...],
                       preferred_element_type=F32) for s in rows]
        ds = [jnp.dot((jax.nn.silu(gu[:, :D_FF]) * gu[:, D_FF:]).astype(BF16), wd_bf[...],
                      preferred_element_type=F32) for gu in gus]
        for s, d in zip(rows, ds):
            o_ref[s, :] = x_ref[s, :] + 0.5 * _rms(d, post_ref[...])


def _ffn(x, layer, pre_g, w_gu, w_down, post_g):
    t = x.shape[0]
    tm = min(TM_FFN, t)
    nc = FFN_CAST_STEPS
    tok = lambda i: (jnp.maximum(i - nc, 0), 0)
    chunk = lambda i: (layer, jnp.minimum(i, nc - 1), 0)
    return pl.pallas_call(
        _ffn_body,
        grid=(nc + t // tm,),
        in_specs=[
            pl.BlockSpec((tm, D_MODEL), tok),
            _layer_spec((1, D_MODEL), layer),
            pl.BlockSpec((None, D_MODEL // nc, 2 * D_FF), chunk),
            pl.BlockSpec((None, D_FF // nc, D_MODEL), chunk),
            _layer_spec((1, D_MODEL), layer),
        ],
        out_specs=pl.BlockSpec((tm, D_MODEL), tok),
        out_shape=jax.ShapeDtypeStruct((t, D_MODEL), F32),
        scratch_shapes=[pltpu.VMEM((D_MODEL, 2 * D_FF), BF16),
                        pltpu.VMEM((D_FF, D_MODEL), BF16)],
        compiler_params=pltpu.CompilerParams(
            dimension_semantics=("arbitrary",), vmem_limit_bytes=VMEM_LIMIT),
        name="ffn",
    )(x, pre_g, w_gu, w_down, post_g)


def _proj_body(x_ref, pre_ref, win_ref, qg_ref, wqt_ref, wqswt_ref, kvg_ref,
               wk_ref, wvt_ref, gg_ref, wscat_ref, gbias_ref, gmask_ref,
               ktab_ref, qcos_ref, qsin_ref,
               p_ref, gb_ref, ygm_ref, qt_ref, k_ref, vt_ref):
    tm = x_ref.shape[0]
    r = tm // PROJ_SUBTILES
    rows = [slice(i * r, (i + 1) * r) for i in range(PROJ_SUBTILES)]
    zs = [jnp.dot(_rms(x_ref[s, :], pre_ref[...]).astype(BF16), win_ref[...],
                  preferred_element_type=F32) for s in rows]
    scale = math.log2(math.e) / math.sqrt(QK_DIM)
    nt = (((1,), (1,)), ((), ()))
    zero_pad = jnp.zeros((HEAD_PAD - QK_DIM, r), BF16)
    ones_rows = (lax.broadcasted_iota(jnp.int32, (V_PAD - V_DIM, r), 0) == 0).astype(BF16)
    wscat = wscat_ref[...]
    gbias = gbias_ref[...]
    gmask = gmask_ref[...] != 0

    for s, z in zip(rows, zs):
        p_ref[s, :] = z[:, C_GC:C_GC + CONV_W] * z[:, C_XC:C_XC + CONV_W]
        gb_ref[s, :] = z[:, C_GB:C_GB + CONV_W]

        cqn = _rms(z[:, C_CQ:C_CQ + Q_RANK], qg_ref[...]).astype(BF16)
        qt = lax.dot_general(wqt_ref[...], cqn, nt, preferred_element_type=F32)
        qswt = lax.dot_general(wqswt_ref[...], cqn, nt, preferred_element_type=F32)
        qcos = qcos_ref[:, s]
        qsin = qsin_ref[:, s]
        for h in range(MLA_HEADS):
            r0 = h * QK_DIM
            nope = qt[r0:r0 + NOPE_DIM] * scale
            rope = (qt[r0 + NOPE_DIM:r0 + QK_DIM] * qcos
                    + qswt[h * ROPE_DIM:(h + 1) * ROPE_DIM] * qsin) * scale
            qt_ref[h, 0:NOPE_DIM, s] = nope.astype(BF16)
            qt_ref[h, NOPE_DIM:QK_DIM, s] = rope.astype(BF16)
            qt_ref[h, QK_DIM:HEAD_PAD, s] = zero_pad

        ckvn = _rms(z[:, C_CKV:C_CKV + KV_RANK], kvg_ref[...]).astype(BF16)
        kr = z[:, C_KR:C_KR + LANES] * ktab_ref[s, :]
        kr = kr + pltpu.roll(kr, LANES - ROPE_DIM, 1)
        kin = jnp.concatenate([ckvn, kr.astype(BF16)], axis=1)
        kall = jnp.dot(kin, wk_ref[...], preferred_element_type=F32).astype(BF16)
        for h in range(MLA_HEADS):
            k_ref[h, s, :] = kall[:, h * HEAD_PAD:(h + 1) * HEAD_PAD]
        vt = lax.dot_general(wvt_ref[...], ckvn, nt, preferred_element_type=F32)
        for h in range(MLA_HEADS):
            vt_ref[h, 0:V_DIM, s] = vt[h * V_DIM:(h + 1) * V_DIM].astype(BF16)
            vt_ref[h, V_DIM:V_PAD, s] = ones_rows

        u = _gelu(z[:, C_ZU:C_ZU + GMLP_W])
        v = _rms(_gelu(z[:, C_ZV:C_ZV + GMLP_W]), gg_ref[...]).astype(BF16)
        for c in range(r // CHUNK):
            vc = v[c * CHUNK:(c + 1) * CHUNK]
            rhs = jnp.where(gmask, jnp.concatenate([vc] * GMLP_HEADS, axis=0),
                            jnp.zeros((), BF16))
            mixed = jnp.dot(wscat, rhs, preferred_element_type=F32) + gbias
            c0 = s.start + c * CHUNK
            ygm_ref[c0:c0 + CHUNK, :] = (u[c * CHUNK:(c + 1) * CHUNK] * mixed).astype(BF16)


def _proj(x, seq, layer, pre_g, w_in, q_g, wq_t, wqsw_t, kv_g, w_k, wv_t, gm_g,
          ws_cat, g_bias, g_mask, k_tab, q_cos, q_sin):
    t = x.shape[0]
    tm = min(TM_PROJ, seq)
    per_seq = seq // tm
    tok = lambda i: (i, 0)
    return pl.pallas_call(
        _proj_body,
        grid=(t // tm,),
        in_specs=[
            pl.BlockSpec((tm, D_MODEL), tok),
            _layer_spec((1, D_MODEL), layer),
            _layer_spec((D_MODEL, N_IN_PAD), layer),
            _layer_spec((1, Q_RANK), layer),
            _layer_spec((MLA_HEADS * QK_DIM, Q_RANK), layer),
            _layer_spec((MLA_HEADS * ROPE_DIM, Q_RANK), layer),
            _layer_spec((1, KV_RANK), layer),
            _layer_spec((2 * LANES, MLA_HEADS * HEAD_PAD), layer),
            _layer_spec((MLA_W, KV_RANK), layer),
            _layer_spec((1, GMLP_W), layer),
            _layer_spec((CHUNK, GMLP_HEADS * CHUNK), layer),
            _layer_spec((CHUNK, GMLP_W), layer),
            _const_spec((GMLP_HEADS * CHUNK, GMLP_W)),
            pl.BlockSpec((tm, LANES), lambda i: (i % per_seq, 0)),
            pl.BlockSpec((ROPE_DIM, tm), lambda i: (0, i % per_seq)),
            pl.BlockSpec((ROPE_DIM, tm), lambda i: (0, i % per_seq)),
        ],
        out_specs=[
            pl.BlockSpec((tm, CONV_W), tok),
            pl.BlockSpec((tm, CONV_W), tok),
            pl.BlockSpec((tm, GMLP_W), tok),
            pl.BlockSpec((MLA_HEADS, HEAD_PAD, tm), lambda i: (0, 0, i)),
            pl.BlockSpec((MLA_HEADS, tm, HEAD_PAD), lambda i: (0, i, 0)),
            pl.BlockSpec((MLA_HEADS, V_PAD, tm), lambda i: (0, 0, i)),
        ],
        out_shape=[
            jax.ShapeDtypeStruct((t, CONV_W), F32),
            jax.ShapeDtypeStruct((t, CONV_W), F32),
            jax.ShapeDtypeStruct((t, GMLP_W), BF16),
            jax.ShapeDtypeStruct((MLA_HEADS, HEAD_PAD, t), BF16),
            jax.ShapeDtypeStruct((MLA_HEADS, t, HEAD_PAD), BF16),
            jax.ShapeDtypeStruct((MLA_HEADS, V_PAD, t), BF16),
        ],
        compiler_params=pltpu.CompilerParams(
            dimension_semantics=("parallel",), vmem_limit_bytes=VMEM_LIMIT),
        name="proj",
    )(x, pre_g, w_in, q_g, wq_t, wqsw_t, kv_g, w_k, wv_t, gm_g, ws_cat,
      g_bias, g_mask, k_tab, q_cos, q_sin)


def _mix_body(x_ref, p_ref, gb_ref, ygm_ref, qt_ref, k_ref, vt_ref,
              cw_ref, cb_ref, wout_ref, post_ref, o_ref, ot_ref):
    tq = x_ref.shape[0]
    seq = p_ref.shape[0]
    r = tq // MIX_SUBTILES
    step = pl.program_id(1)
    last_step = pl.num_programs(1) - 1

    def finish(j):
        rows = slice(j * r, (j + 1) * r)
        r0 = pl.multiple_of(step * tq + j * r, r)
        cur = p_ref[pl.ds(r0, r), :]
        lo = pl.multiple_of(jnp.maximum(r0 - 8, 0), 8)
        hi = pl.multiple_of(jnp.minimum(r0 + r, seq - 8), 8)
        before = p_ref[pl.ds(lo, 8), :][7:8, :]
        after = p_ref[pl.ds(hi, 8), :][0:1, :]
        if j == 0:
            before = jnp.where(step > 0, before, 0.0)
        if j == MIX_SUBTILES - 1:
            after = jnp.where(step < last_step, after, 0.0)
        row = lax.broadcasted_iota(jnp.int32, (r, CONV_W), 0)
        prev = jnp.where(row == 0, before, pltpu.roll(cur, 1, 0))
        nxt = jnp.where(row == r - 1, after, pltpu.roll(cur, r - 1, 0))
        cw = cw_ref[...]
        conv = prev * cw[0:1, :] + cur * cw[1:2, :] + nxt * cw[2:3, :]
        y_conv = gb_ref[rows, :] * (conv + cb_ref[...])
        y_mla = ot_ref[j].T
        ycat = jnp.concatenate(
            [y_conv.astype(BF16), y_mla.astype(BF16), ygm_ref[rows, :]], axis=1)
        y = jnp.dot(ycat, wout_ref[...], preferred_element_type=F32)
        o_ref[rows, :] = x_ref[rows, :] + _rms(y, post_ref[...])

    kc = min(KEY_CHUNK, seq)
    nc = seq // kc
    units = [(j, h, c) for j in range(MIX_SUBTILES)
             for h in range(MLA_HEADS) for c in range(nc)]

    def scores(unit):
        j, h, c = unit
        return jnp.dot(k_ref[h, c * kc:(c + 1) * kc, :], qt_ref[h, :, j * r:(j + 1) * r],
                       preferred_element_type=F32)

    pending = [scores(u) for u in units[:SCORE_LOOKAHEAD]]
    m = acc = None
    for idx, (j, h, c) in enumerate(units):
        if idx + SCORE_LOOKAHEAD < len(units):
            pending.append(scores(units[idx + SCORE_LOOKAHEAD]))
        s = pending.pop(0)
        m_c = jnp.max(s, axis=0, keepdims=True)
        m_new = m_c if c == 0 else jnp.maximum(m, m_c)
        e = jnp.exp2(s - m_new)
        pv = jnp.dot(vt_ref[h, :, c * kc:(c + 1) * kc], e.astype(BF16),
                     preferred_element_type=F32)
        acc = pv if c == 0 else jnp.exp2(m - m_new) * acc + pv
        m = m_new
        if c == nc - 1:
            ot_ref[j, h * V_DIM:(h + 1) * V_DIM, :] = acc[0:V_DIM] / acc[V_DIM:V_DIM + 1]
            if h == MLA_HEADS - 1:
                finish(j)


def _mix(x, batch, seq, layer, p, gb, ygm, qt, k, vt, conv_w, conv_b, w_out, post_g):
    t = x.shape[0]
    tq = min(TQ_MIX, seq)
    nq = seq // tq
    tok = lambda b, i: (b * nq + i, 0)
    return pl.pallas_call(
        _mix_body,
        grid=(batch, nq),
        in_specs=[
            pl.BlockSpec((tq, D_MODEL), tok),
            pl.BlockSpec((seq, CONV_W), lambda b, i: (b, 0)),
            pl.BlockSpec((tq, CONV_W), tok),
            pl.BlockSpec((tq, GMLP_W), tok),
            pl.BlockSpec((MLA_HEADS, HEAD_PAD, tq), lambda b, i: (0, 0, b * nq + i)),
            pl.BlockSpec((MLA_HEADS, seq, HEAD_PAD), lambda b, i: (0, b, 0)),
            pl.BlockSpec((MLA_HEADS, V_PAD, seq), lambda b, i: (0, 0, b)),
            _layer_spec((8, CONV_W), layer),
            _layer_spec((1, CONV_W), layer),
            _layer_spec((D_MODEL, D_MODEL), layer),
            _layer_spec((1, D_MODEL), layer),
        ],
        out_specs=pl.BlockSpec((tq, D_MODEL), tok),
        out_shape=jax.ShapeDtypeStruct((t, D_MODEL), F32),
        scratch_shapes=[pltpu.VMEM((MIX_SUBTILES, MLA_W, tq // MIX_SUBTILES), F32)],
        compiler_params=pltpu.CompilerParams(
            dimension_semantics=("parallel", "arbitrary"),
            vmem_limit_bytes=VMEM_LIMIT),
        name="mix",
    )(x, p, gb, ygm, qt, k, vt, conv_w, conv_b, w_out, post_g)


def _rope_tables(seq):
    pos = jnp.arange(seq, dtype=F32)
    inv = 1.0 / (ROPE_THETA ** (jnp.arange(0, ROPE_DIM, 2, dtype=F32) / ROPE_DIM))
    ang = pos[:, None] * inv[None, :]
    cos, sin = jnp.cos(ang), jnp.sin(ang)
    cc = jnp.concatenate([cos, cos], axis=1)
    ss = jnp.concatenate([-sin, sin], axis=1)
    k_tab = jnp.concatenate(
        [cc, ss, jnp.zeros((seq, LANES - 2 * ROPE_DIM), F32)], axis=1)
    return k_tab, cc.T, ss.T


def _half_swap(w, axis):
    a, b = jnp.split(w, 2, axis=axis)
    return jnp.concatenate([b, a], axis=axis)


def _prep_params(w_in, w_uq, w_ukv, gmlp_ws, gmlp_b, conv_w):
    depth = w_in.shape[0]
    s = np.cumsum([CONV_W, CONV_W, CONV_W, Q_RANK, KV_RANK, ROPE_DIM, GMLP_W]).tolist()
    xc, gb, gc, cq, ckv, kr, zu, zv = jnp.split(w_in, s, axis=2)
    pad = jnp.zeros((depth, D_MODEL, N_IN_PAD - N_IN - ROPE_DIM), F32)
    w_in_p = jnp.concatenate(
        [xc, gb, gc, cq, zu, zv, ckv, kr, _half_swap(kr, 2), pad], axis=2).astype(BF16)

    wq = w_uq.reshape(depth, Q_RANK, MLA_HEADS, QK_DIM)
    wq_t = jnp.swapaxes(w_uq, 1, 2).astype(BF16)
    wqsw_t = jnp.swapaxes(_half_swap(wq[..., NOPE_DIM:], 3).reshape(
        depth, Q_RANK, MLA_HEADS * ROPE_DIM), 1, 2).astype(BF16)

    wkv = w_ukv.reshape(depth, KV_RANK, MLA_HEADS, NOPE_DIM + V_DIM)
    wk_nope = jnp.pad(wkv[..., :NOPE_DIM],
                      ((0, 0), (0, 0), (0, 0), (0, HEAD_PAD - NOPE_DIM)))
    place = np.zeros((LANES, MLA_HEADS, HEAD_PAD), np.float32)
    j = np.arange(ROPE_DIM)
    place[j, :, NOPE_DIM + j] = 1.0
    place = jnp.broadcast_to(jnp.asarray(place), (depth,) + place.shape)
    w_k = jnp.concatenate([wk_nope, place], axis=1).reshape(
        depth, 2 * LANES, MLA_HEADS * HEAD_PAD).astype(BF16)
    wv_t = jnp.swapaxes(wkv[..., NOPE_DIM:].reshape(depth, KV_RANK, MLA_W),
                        1, 2).astype(BF16)

    ws_cat = jnp.transpose(gmlp_ws, (0, 2, 1, 3)).reshape(
        depth, CHUNK, GMLP_HEADS * CHUNK).astype(BF16)
    g_bias = jnp.repeat(jnp.swapaxes(gmlp_b, 1, 2), GMLP_HD, axis=2)
    rg = np.arange(GMLP_HEADS * CHUNK)[:, None] // CHUNK
    cg = np.arange(GMLP_W)[None, :] // GMLP_HD
    g_mask = jnp.asarray(rg == cg, BF16)
    conv_w8 = jnp.pad(conv_w, ((0, 0), (0, 8 - conv_w.shape[1]), (0, 0)))
    return w_in_p, wq_t, wqsw_t, w_k, wv_t, ws_cat, g_bias, g_mask, conv_w8


def kernel(x, ffn1_pre_g, ffn1_w_gu, ffn1_w_down, ffn1_post_g, mix_pre_g, w_in,
           conv_w, conv_b, q_norm_g, w_uq, kv_norm_g, w_ukv, gmlp_norm_g, gmlp_ws,
           gmlp_b, w_out, mix_post_g, ffn2_pre_g, ffn2_w_gu, ffn2_w_down, ffn2_post_g):
    batch, seq, _ = x.shape
    depth = w_in.shape[0]
    k_tab, q_cos, q_sin = _rope_tables(seq)
    rows = lambda g: g[:, None, :]
    (w_in_p, wq_t, wqsw_t, w_k, wv_t, ws_cat, g_bias, g_mask, conv_w8) = _prep_params(
        w_in, w_uq, w_ukv, gmlp_ws, gmlp_b, conv_w)
    ffn1 = (rows(ffn1_pre_g), ffn1_w_gu, ffn1_w_down, rows(ffn1_post_g))
    ffn2 = (rows(ffn2_pre_g), ffn2_w_gu, ffn2_w_down, rows(ffn2_post_g))
    w_out_b = w_out.astype(BF16)
    h = x.reshape(batch * seq, D_MODEL)
    for l in range(depth):
        h = _ffn(h, l, *ffn1)
        p, gb, ygm, qt, k, vt = _proj(
            h, seq, l, rows(mix_pre_g), w_in_p, rows(q_norm_g), wq_t, wqsw_t,
            rows(kv_norm_g), w_k, wv_t, rows(gmlp_norm_g), ws_cat, g_bias,
            g_mask, k_tab, q_cos, q_sin)
        h = _mix(h, batch, seq, l, p, gb, ygm, qt, k, vt, conv_w8, rows(conv_b),
                 w_out_b, rows(mix_post_g))
        h = _ffn(h, l, *ffn2)
    return h.reshape(batch, seq, D_MODEL)
```

```python
import math

import jax
import jax.numpy as jnp
import numpy as np
from jax import lax
from jax.experimental import pallas as pl
from jax.experimental.pallas import tpu as pltpu

F32 = jnp.float32
BF16 = jnp.bfloat16

D_MODEL = 1024
D_FF = 2816
EPS = 1e-6
CONV_W = 256
MLA_HEADS = 8
NOPE_DIM = 64
ROPE_DIM = 32
V_DIM = 64
Q_RANK = 256
KV_RANK = 128
QK_DIM = NOPE_DIM + ROPE_DIM
MLA_W = MLA_HEADS * V_DIM
ROPE_THETA = 10000.0
GMLP_W = 256
GMLP_HEADS = 4
GMLP_HD = 64
CHUNK = 128
N_IN = 3 * CONV_W + Q_RANK + KV_RANK + ROPE_DIM + 2 * GMLP_W

LANES = 128
HEAD_PAD = LANES
V_PAD = V_DIM + 16
C_XC, C_GB, C_GC, C_CQ, C_ZU, C_ZV, C_CKV, C_KR = 0, 256, 512, 768, 1024, 1280, 1536, 1664
N_IN_PAD = 1792

TM_FFN = 1024
FFN_SUBTILES = 4
FFN_CAST_STEPS = 8
TM_PROJ = 1024
PROJ_SUBTILES = 4
TQ_MIX = 1024
MIX_SUBTILES = 4
KEY_CHUNK = 256
SCORE_LOOKAHEAD = 5
VMEM_LIMIT = 56 * 1024 * 1024


def _rms(x, g):
    return x * lax.rsqrt(jnp.mean(x * x, axis=-1, keepdims=True) + EPS) * g


def _gelu(x):
    return 0.5 * x * (1.0 + lax.erf(x * math.sqrt(0.5)))


def _const_spec(shape):
    nd = len(shape)
    return pl.BlockSpec(shape, lambda *_: (0,) * nd, pipeline_mode=pl.Buffered(1))


def _layer_spec(shape, layer):
    nd = len(shape)
    return pl.BlockSpec((None,) + tuple(shape), lambda *_: (layer,) + (0,) * nd,
                        pipeline_mode=pl.Buffered(1))


def _ffn_body(x_ref, pre_ref, wgu_ref, wd_ref, post_ref, o_ref, wgu_bf, wd_bf):
    step = pl.program_id(0)

    @pl.when(step < FFN_CAST_STEPS)
    def _():
        gr, dr = wgu_ref.shape[0], wd_ref.shape[0]
        wgu_bf[pl.ds(pl.multiple_of(step * gr, gr), gr), :] = wgu_ref[...].astype(BF16)
        wd_bf[pl.ds(pl.multiple_of(step * dr, dr), dr), :] = wd_ref[...].astype(BF16)

    @pl.when(step >= FFN_CAST_STEPS)
    def _():
        tm = x_ref.shape[0]
        r = tm // FFN_SUBTILES
        rows = [slice(i * r, (i + 1) * r) for i in range(FFN_SUBTILES)]
        gus = [jnp.dot(_rms(x_ref[s, :], pre_ref[...]).astype(BF16), wgu_bf[...],
                       preferred_element_type=F32) for s in rows]
        ds = [jnp.dot((jax.nn.silu(gu[:, :D_FF]) * gu[:, D_FF:]).astype(BF16), wd_bf[...],
                      preferred_element_type=F32) for gu in gus]
        for s, d in zip(rows, ds):
            o_ref[s, :] = x_ref[s, :] + 0.5 * _rms(d, post_ref[...])


def _ffn(x, layer, pre_g, w_gu, w_down, post_g):
    t = x.shape[0]
    tm = min(TM_FFN, t)
    nc = FFN_CAST_STEPS
    tok = lambda i: (jnp.maximum(i - nc, 0), 0)
    chunk = lambda i: (layer, jnp.minimum(i, nc - 1), 0)
    return pl.pallas_call(
        _ffn_body,
        grid=(nc + t // tm,),
        in_specs=[
            pl.BlockSpec((tm, D_MODEL), tok),
            _layer_spec((1, D_MODEL), layer),
            pl.BlockSpec((None, D_MODEL // nc, 2 * D_FF), chunk),
            pl.BlockSpec((None, D_FF // nc, D_MODEL), chunk),
            _layer_spec((1, D_MODEL), layer),
        ],
        out_specs=pl.BlockSpec((tm, D_MODEL), tok),
        out_shape=jax.ShapeDtypeStruct((t, D_MODEL), F32),
        scratch_shapes=[pltpu.VMEM((D_MODEL, 2 * D_FF), BF16),
                        pltpu.VMEM((D_FF, D_MODEL), BF16)],
        compiler_params=pltpu.CompilerParams(
            dimension_semantics=("arbitrary",), vmem_limit_bytes=VMEM_LIMIT),
        name="ffn",
    )(x, pre_g, w_gu, w_down, post_g)


def _proj_body(x_ref, pre_ref, win_ref, qg_ref, wqt_ref, wqswt_ref, kvg_ref,
               wk_ref, wvt_ref, gg_ref, wscat_ref, gbias_ref, gmask_ref,
               ktab_ref, qcos_ref, qsin_ref,
               p_ref, gb_ref, ygm_ref, qt_ref, k_ref, vt_ref):
    tm = x_ref.shape[0]
    r = tm // PROJ_SUBTILES
    rows = [slice(i * r, (i + 1) * r) for i in range(PROJ_SUBTILES)]
    zs = [jnp.dot(_rms(x_ref[s, :], pre_ref[...]).astype(BF16), win_ref[...],
                  preferred_element_type=F32) for s in rows]
    scale = math.log2(math.e) / math.sqrt(QK_DIM)
    nt = (((1,), (1,)), ((), ()))
    zero_pad = jnp.zeros((HEAD_PAD - QK_DIM, r), BF16)
    ones_rows = (lax.broadcasted_iota(jnp.int32, (V_PAD - V_DIM, r), 0) == 0).astype(BF16)
    wscat = wscat_ref[...]
    gbias = gbias_ref[...]
    gmask = gmask_ref[...] != 0

    for s, z in zip(rows, zs):
        p_ref[s, :] = z[:, C_GC:C_GC + CONV_W] * z[:, C_XC:C_XC + CONV_W]
        gb_ref[s, :] = z[:, C_GB:C_GB + CONV_W]

        cqn = _rms(z[:, C_CQ:C_CQ + Q_RANK], qg_ref[...]).astype(BF16)
        qt = lax.dot_general(wqt_ref[...], cqn, nt, preferred_element_type=F32)
        qswt = lax.dot_general(wqswt_ref[...], cqn, nt, preferred_element_type=F32)
        qcos = qcos_ref[:, s]
        qsin = qsin_ref[:, s]
        for h in range(MLA_HEADS):
            r0 = h * QK_DIM
            nope = qt[r0:r0 + NOPE_DIM] * scale
            rope = (qt[r0 + NOPE_DIM:r0 + QK_DIM] * qcos
                    + qswt[h * ROPE_DIM:(h + 1) * ROPE_DIM] * qsin) * scale
            qt_ref[h, 0:NOPE_DIM, s] = nope.astype(BF16)
            qt_ref[h, NOPE_DIM:QK_DIM, s] = rope.astype(BF16)
            qt_ref[h, QK_DIM:HEAD_PAD, s] = zero_pad

        ckvn = _rms(z[:, C_CKV:C_CKV + KV_RANK], kvg_ref[...]).astype(BF16)
        kr = z[:, C_KR:C_KR + LANES] * ktab_ref[s, :]
        kr = kr + pltpu.roll(kr, LANES - ROPE_DIM, 1)
        kin = jnp.concatenate([ckvn, kr.astype(BF16)], axis=1)
        kall = jnp.dot(kin, wk_ref[...], preferred_element_type=F32).astype(BF16)
        for h in range(MLA_HEADS):
            k_ref[h, s, :] = kall[:, h * HEAD_PAD:(h + 1) * HEAD_PAD]
        vt = lax.dot_general(wvt_ref[...], ckvn, nt, preferred_element_type=F32)
        for h in range(MLA_HEADS):
            vt_ref[h, 0:V_DIM, s] = vt[h * V_DIM:(h + 1) * V_DIM].astype(BF16)
            vt_ref[h, V_DIM:V_PAD, s] = ones_rows

        u = _gelu(z[:, C_ZU:C_ZU + GMLP_W])
        v = _rms(_gelu(z[:, C_ZV:C_ZV + GMLP_W]), gg_ref[...]).astype(BF16)
        for c in range(r // CHUNK):
            vc = v[c * CHUNK:(c + 1) * CHUNK]
            rhs = jnp.where(gmask, jnp.concatenate([vc] * GMLP_HEADS, axis=0),
                            jnp.zeros((), BF16))
            mixed = jnp.dot(wscat, rhs, preferred_element_type=F32) + gbias
            c0 = s.start + c * CHUNK
            ygm_ref[c0:c0 + CHUNK, :] = (u[c * CHUNK:(c + 1) * CHUNK] * mixed).astype(BF16)


def _proj(x, seq, layer, pre_g, w_in, q_g, wq_t, wqsw_t, kv_g, w_k, wv_t, gm_g,
          ws_cat, g_bias, g_mask, k_tab, q_cos, q_sin):
    t = x.shape[0]
    tm = min(TM_PROJ, seq)
    per_seq = seq // tm
    tok = lambda i: (i, 0)
    return pl.pallas_call(
        _proj_body,
        grid=(t // tm,),
        in_specs=[
            pl.BlockSpec((tm, D_MODEL), tok),
            _layer_spec((1, D_MODEL), layer),
            _layer_spec((D_MODEL, N_IN_PAD), layer),
            _layer_spec((1, Q_RANK), layer),
            _layer_spec((MLA_HEADS * QK_DIM, Q_RANK), layer),
            _layer_spec((MLA_HEADS * ROPE_DIM, Q_RANK), layer),
            _layer_spec((1, KV_RANK), layer),
            _layer_spec((2 * LANES, MLA_HEADS * HEAD_PAD), layer),
            _layer_spec((MLA_W, KV_RANK), layer),
            _layer_spec((1, GMLP_W), layer),
            _layer_spec((CHUNK, GMLP_HEADS * CHUNK), layer),
            _layer_spec((CHUNK, GMLP_W), layer),
            _const_spec((GMLP_HEADS * CHUNK, GMLP_W)),
            pl.BlockSpec((tm, LANES), lambda i: (i % per_seq, 0)),
            pl.BlockSpec((ROPE_DIM, tm), lambda i: (0, i % per_seq)),
            pl.BlockSpec((ROPE_DIM, tm), lambda i: (0, i % per_seq)),
        ],
        out_specs=[
            pl.BlockSpec((tm, CONV_W), tok),
            pl.BlockSpec((tm, CONV_W), tok),
            pl.BlockSpec((tm, GMLP_W), tok),
            pl.BlockSpec((MLA_HEADS, HEAD_PAD, tm), lambda i: (0, 0, i)),
            pl.BlockSpec((MLA_HEADS, tm, HEAD_PAD), lambda i: (0, i, 0)),
            pl.BlockSpec((MLA_HEADS, V_PAD, tm), lambda i: (0, 0, i)),
        ],
        out_shape=[
            jax.ShapeDtypeStruct((t, CONV_W), F32),
            jax.ShapeDtypeStruct((t, CONV_W), F32),
            jax.ShapeDtypeStruct((t, GMLP_W), BF16),
            jax.ShapeDtypeStruct((MLA_HEADS, HEAD_PAD, t), BF16),
            jax.ShapeDtypeStruct((MLA_HEADS, t, HEAD_PAD), BF16),
            jax.ShapeDtypeStruct((MLA_HEADS, V_PAD, t), BF16),
        ],
        compiler_params=pltpu.CompilerParams(
            dimension_semantics=("parallel",), vmem_limit_bytes=VMEM_LIMIT),
        name="proj",
    )(x, pre_g, w_in, q_g, wq_t, wqsw_t, kv_g, w_k, wv_t, gm_g, ws_cat,
      g_bias, g_mask, k_tab, q_cos, q_sin)


def _mix_body(x_ref, p_ref, gb_ref, ygm_ref, qt_ref, k_ref, vt_ref,
              cw_ref, cb_ref, wout_ref, post_ref, o_ref, ot_ref):
    tq = x_ref.shape[0]
    seq = p_ref.shape[0]
    r = tq // MIX_SUBTILES
    step = pl.program_id(1)
    last_step = pl.num_programs(1) - 1

    def finish(j):
        rows = slice(j * r, (j + 1) * r)
        r0 = pl.multiple_of(step * tq + j * r, r)
        cur = p_ref[pl.ds(r0, r), :]
        lo = pl.multiple_of(jnp.maximum(r0 - 8, 0), 8)
        hi = pl.multiple_of(jnp.minimum(r0 + r, seq - 8), 8)
        before = p_ref[pl.ds(lo, 8), :][7:8, :]
        after = p_ref[pl.ds(hi, 8), :][0:1, :]
        if j == 0:
            before = jnp.where(step > 0, before, 0.0)
        if j == MIX_SUBTILES - 1:
            after = jnp.where(step < last_step, after, 0.0)
        row = lax.broadcasted_iota(jnp.int32, (r, CONV_W), 0)
        prev = jnp.where(row == 0, before, pltpu.roll(cur, 1, 0))
        nxt = jnp.where(row == r - 1, after, pltpu.roll(cur, r - 1, 0))
        cw = cw_ref[...]
        conv = prev * cw[0:1, :] + cur * cw[1:2, :] + nxt * cw[2:3, :]
        y_conv = gb_ref[rows, :] * (conv + cb_ref[...])
        y_mla = ot_ref[j].T
        ycat = jnp.concatenate(
            [y_conv.astype(BF16), y_mla.astype(BF16), ygm_ref[rows, :]], axis=1)
        y = jnp.dot(ycat, wout_ref[...], preferred_element_type=F32)
        o_ref[rows, :] = x_ref[rows, :] + _rms(y, post_ref[...])

    kc = min(KEY_CHUNK, seq)
    nc = seq // kc
    units = [(j, h, c) for j in range(MIX_SUBTILES)
             for h in range(MLA_HEADS) for c in range(nc)]

    def scores(unit):
        j, h, c = unit
        return jnp.dot(k_ref[h, c * kc:(c + 1) * kc, :], qt_ref[h, :, j * r:(j + 1) * r],
                       preferred_element_type=F32)

    pending = [scores(u) for u in units[:SCORE_LOOKAHEAD]]
    m = acc = None
    for idx, (j, h, c) in enumerate(units):
        if idx + SCORE_LOOKAHEAD < len(units):
            pending.append(scores(units[idx + SCORE_LOOKAHEAD]))
        s = pending.pop(0)
        m_c = jnp.max(s, axis=0, keepdims=True)
        m_new = m_c if c == 0 else jnp.maximum(m, m_c)
        e = jnp.exp2(s - m_new)
        pv = jnp.dot(vt_ref[h, :, c * kc:(c + 1) * kc], e.astype(BF16),
                     preferred_element_type=F32)
        acc = pv if c == 0 else jnp.exp2(m - m_new) * acc + pv
        m = m_new
        if c == nc - 1:
            ot_ref[j, h * V_DIM:(h + 1) * V_DIM, :] = acc[0:V_DIM] / acc[V_DIM:V_DIM + 1]
            if h == MLA_HEADS - 1:
                finish(j)


def _mix(x, batch, seq, layer, p, gb, ygm, qt, k, vt, conv_w, conv_b, w_out, post_g):
    t = x.shape[0]
    tq = min(TQ_MIX, seq)
    nq = seq // tq
    tok = lambda b, i: (b * nq + i, 0)
    return pl.pallas_call(
        _mix_body,
        grid=(batch, nq),
        in_specs=[
            pl.BlockSpec((tq, D_MODEL), tok),
            pl.BlockSpec((seq, CONV_W), lambda b, i: (b, 0)),
            pl.BlockSpec((tq, CONV_W), tok),
            pl.BlockSpec((tq, GMLP_W), tok),
            pl.BlockSpec((MLA_HEADS, HEAD_PAD, tq), lambda b, i: (0, 0, b * nq + i)),
            pl.BlockSpec((MLA_HEADS, seq, HEAD_PAD), lambda b, i: (0, b, 0)),
            pl.BlockSpec((MLA_HEADS, V_PAD, seq), lambda b, i: (0, 0, b)),
            _layer_spec((8, CONV_W), layer),
            _layer_spec((1, CONV_W), layer),
            _layer_spec((D_MODEL, D_MODEL), layer),
            _layer_spec((1, D_MODEL), layer),
        ],
        out_specs=pl.BlockSpec((tq, D_MODEL), tok),
        out_shape=jax.ShapeDtypeStruct((t, D_MODEL), F32),
        scratch_shapes=[pltpu.VMEM((MIX_SUBTILES, MLA_W, tq // MIX_SUBTILES), F32)],
        compiler_params=pltpu.CompilerParams(
            dimension_semantics=("parallel", "arbitrary"),
            vmem_limit_bytes=VMEM_LIMIT),
        name="mix",
    )(x, p, gb, ygm, qt, k, vt, conv_w, conv_b, w_out, post_g)


def _rope_tables(seq):
    pos = jnp.arange(seq, dtype=F32)
    inv = 1.0 / (ROPE_THETA ** (jnp.arange(0, ROPE_DIM, 2, dtype=F32) / ROPE_DIM))
    ang = pos[:, None] * inv[None, :]
    cos, sin = jnp.cos(ang), jnp.sin(ang)
    cc = jnp.concatenate([cos, cos], axis=1)
    ss = jnp.concatenate([-sin, sin], axis=1)
    k_tab = jnp.concatenate(
        [cc, ss, jnp.zeros((seq, LANES - 2 * ROPE_DIM), F32)], axis=1)
    return k_tab, cc.T, ss.T


def _half_swap(w, axis):
    a, b = jnp.split(w, 2, axis=axis)
    return jnp.concatenate([b, a], axis=axis)


def _prep_params(w_in, w_uq, w_ukv, gmlp_ws, gmlp_b, conv_w):
    depth = w_in.shape[0]
    s = np.cumsum([CONV_W, CONV_W, CONV_W, Q_RANK, KV_RANK, ROPE_DIM, GMLP_W]).tolist()
    xc, gb, gc, cq, ckv, kr, zu, zv = jnp.split(w_in, s, axis=2)
    pad = jnp.zeros((depth, D_MODEL, N_IN_PAD - N_IN - ROPE_DIM), F32)
    w_in_p = jnp.concatenate(
        [xc, gb, gc, cq, zu, zv, ckv, kr, _half_swap(kr, 2), pad], axis=2).astype(BF16)

    wq = w_uq.reshape(depth, Q_RANK, MLA_HEADS, QK_DIM)
    wq_t = jnp.swapaxes(w_uq, 1, 2).astype(BF16)
    wqsw_t = jnp.swapaxes(_half_swap(wq[..., NOPE_DIM:], 3).reshape(
        depth, Q_RANK, MLA_HEADS * ROPE_DIM), 1, 2).astype(BF16)

    wkv = w_ukv.reshape(depth, KV_RANK, MLA_HEADS, NOPE_DIM + V_DIM)
    wk_nope = jnp.pad(wkv[..., :NOPE_DIM],
                      ((0, 0), (0, 0), (0, 0), (0, HEAD_PAD - NOPE_DIM)))
    place = np.zeros((LANES, MLA_HEADS, HEAD_PAD), np.float32)
    j = np.arange(ROPE_DIM)
    place[j, :, NOPE_DIM + j] = 1.0
    place = jnp.broadcast_to(jnp.asarray(place), (depth,) + place.shape)
    w_k = jnp.concatenate([wk_nope, place], axis=1).reshape(
        depth, 2 * LANES, MLA_HEADS * HEAD_PAD).astype(BF16)
    wv_t = jnp.swapaxes(wkv[..., NOPE_DIM:].reshape(depth, KV_RANK, MLA_W),
                        1, 2).astype(BF16)

    ws_cat = jnp.transpose(gmlp_ws, (0, 2, 1, 3)).reshape(
        depth, CHUNK, GMLP_HEADS * CHUNK).astype(BF16)
    g_bias = jnp.repeat(jnp.swapaxes(gmlp_b, 1, 2), GMLP_HD, axis=2)
    rg = np.arange(GMLP_HEADS * CHUNK)[:, None] // CHUNK
    cg = np.arange(GMLP_W)[None, :] // GMLP_HD
    g_mask = jnp.asarray(rg == cg, BF16)
    conv_w8 = jnp.pad(conv_w, ((0, 0), (0, 8 - conv_w.shape[1]), (0, 0)))
    return w_in_p, wq_t, wqsw_t, w_k, wv_t, ws_cat, g_bias, g_mask, conv_w8


def kernel(x, ffn1_pre_g, ffn1_w_gu, ffn1_w_down, ffn1_post_g, mix_pre_g, w_in,
           conv_w, conv_b, q_norm_g, w_uq, kv_norm_g, w_ukv, gmlp_norm_g, gmlp_ws,
           gmlp_b, w_out, mix_post_g, ffn2_pre_g, ffn2_w_gu, ffn2_w_down, ffn2_post_g):
    batch, seq, _ = x.shape
    depth = w_in.shape[0]
    k_tab, q_cos, q_sin = _rope_tables(seq)
    rows = lambda g: g[:, None, :]
    (w_in_p, wq_t, wqsw_t, w_k, wv_t, ws_cat, g_bias, g_mask, conv_w8) = _prep_params(
        w_in, w_uq, w_ukv, gmlp_ws, gmlp_b, conv_w)
    ffn1 = (rows(ffn1_pre_g), ffn1_w_gu, ffn1_w_down, rows(ffn1_post_g))
    ffn2 = (rows(ffn2_pre_g), ffn2_w_gu, ffn2_w_down, rows(ffn2_post_g))
    w_out_b = w_out.astype(BF16)
    h = x.reshape(batch * seq, D_MODEL)
    for l in range(depth):
        h = _ffn(h, l, *ffn1)
        p, gb, ygm, qt, k, vt = _proj(
            h, seq, l, rows(mix_pre_g), w_in_p, rows(q_norm_g), wq_t, wqsw_t,
            rows(kv_norm_g), w_k, wv_t, rows(gmlp_norm_g), ws_cat, g_bias,
            g_mask, k_tab, q_cos, q_sin)
        h = _mix(h, batch, seq, l, p, gb, ygm, qt, k, vt, conv_w8, rows(conv_b),
                 w_out_b, rows(mix_post_g))
        h = _ffn(h, l, *ffn2)
    return h.reshape(batch, seq, D_MODEL)
```
